```python
import math
import jax, jax.numpy as jnp
from jax import lax
import numpy as np

D_MODEL = 1024
BATCH = 32
SEQ = 2048
DEPTH = 4

GRID_W = 64
CTX_LEN = 256
LN_EPS = 1e-5
NORM_EPS = 1e-6
DEEPNORM_ALPHA = (2.0 * DEPTH) ** 0.25
DEEPNORM_BETA = (8.0 * DEPTH) ** -0.25
CHUNK = 64
N_MIXERS = 3

GLA_HEADS = 4
GLA_DK = D_MODEL // (2 * GLA_HEADS)
GLA_DV = D_MODEL // GLA_HEADS
GLA_KEY = GLA_HEADS * GLA_DK
GLA_VAL = GLA_HEADS * GLA_DV
GLA_RANK = 16
GLA_TAU = 16.0

DN_HEADS = 8
DN_DK = D_MODEL // DN_HEADS
DN_DV = D_MODEL // DN_HEADS
DN_KEY = DN_HEADS * DN_DK
DN_VAL = DN_HEADS * DN_DV
DN_CONV = 3

HY_ORDER = 2
HY_SIDES = 2
HY_BANDS = 16
HY_PE_DIM = 1 + 2 * HY_BANDS
HY_FFN = 64
HY_CONV = 3
HY_SHIFT = 0.05
HY_DECAY_MIN = -math.log(1e-2) / 1.5
HY_DECAY_MAX = -math.log(1e-2) / 0.3

FFN_DENSE = 2816
N_EXPERTS = 8
TOP_K = 2
FFN_EXPERT = 3584

N_GLA_LAYERS = (DEPTH + 2) // 3
N_DN_LAYERS = (DEPTH + 1) // 3
N_HY_LAYERS = DEPTH // 3
N_DENSE_LAYERS = (DEPTH + 1) // 2
N_MOE_LAYERS = DEPTH // 2

kernel_name = 'hybrid_diffusion_trunk'
F32 = jnp.float32


def _layer_norm(x, g, b):
    xf = x.astype(F32)
    xc = xf - jnp.mean(xf, -1, keepdims=True)
    var = jnp.mean(xc * xc, -1, keepdims=True)
    return (xc * lax.rsqrt(var + LN_EPS) * g.astype(F32) + b.astype(F32)).astype(x.dtype)


def _rms_norm(x, g):
    xf = x.astype(F32)
    return (xf * lax.rsqrt(jnp.mean(xf * xf, -1, keepdims=True) + NORM_EPS) * g.astype(F32)).astype(x.dtype)


def _l2norm(x):
    xf = x.astype(F32)
    return xf * lax.rsqrt(jnp.sum(xf * xf, -1, keepdims=True) + NORM_EPS)


def _heads(t, h):
    B, L, _ = t.shape
    return t.reshape(B, L, h, -1).transpose(0, 2, 1, 3)


def _merge(t):
    B, H, L, d = t.shape
    return t.transpose(0, 2, 1, 3).reshape(B, L, H * d)


def _flip(t):
    return jnp.flip(t, axis=2)


def _dwconv(x, w):
    K, C = w.shape
    return lax.conv_general_dilated(x, w[:, None, :].astype(x.dtype), window_strides=(1,),
                                    padding=[((K - 1) // 2, K // 2)],
                                    dimension_numbers=('NWC', 'WIO', 'NWC'),
                                    feature_group_count=C)


def _to_chunks(t):
    B, H, L = t.shape[:3]
    t = t.reshape(B, H, L // CHUNK, CHUNK, *t.shape[3:])
    return jnp.moveaxis(t, 2, 0)


def _from_chunks(t):
    N, B, H, C = t.shape[:4]
    return jnp.moveaxis(t, 0, 2).reshape(B, H, N * C, *t.shape[4:])


def _grid_sincos(rows, dtype):
    r, col = jnp.meshgrid(jnp.arange(rows, dtype=F32), jnp.arange(GRID_W, dtype=F32), indexing='ij')
    n_freq = D_MODEL // 4
    omega = 1.0 / (10000.0 ** (jnp.arange(n_freq, dtype=F32) / n_freq))

    def emb(p):
        ang = p.reshape(-1)[:, None] * omega[None, :]
        return jnp.concatenate([jnp.sin(ang), jnp.cos(ang)], -1)

    return jnp.concatenate([emb(r), emb(col)], -1).astype(dtype)


def _gla_scan(q, k, v, log_a, S0, with_output):
    q, k, v = (_to_chunks(t.astype(F32)) for t in (q, k, v))
    b = jnp.cumsum(_to_chunks(log_a), axis=3)
    tril = jnp.tril(jnp.ones((CHUNK, CHUNK), dtype=bool))

    def step(S, inp):
        qc, kc, vc, bc = inp
        b_last = bc[:, :, -1:, :]
        S_new = jnp.exp(b_last[:, :, 0, :, None]) * S + jnp.einsum(
            'bhcd,bhce->bhde', kc * jnp.exp(b_last - bc), vc)
        if not with_output:
            return S_new, None
        q_dec = qc * jnp.exp(bc)
        scores = jnp.where(tril, jnp.einsum('bhcd,bhsd->bhcs', q_dec, kc * jnp.exp(-bc)), 0.0)
        o = jnp.einsum('bhcs,bhse->bhce', scores, vc) + jnp.einsum('bhcd,bhde->bhce', q_dec, S)
        return S_new, o

    S_fin, o = lax.scan(step, S0, (q, k, v, b))
    return (_from_chunks(o) if with_output else None), S_fin


def _gla_mixer(u_c, u_x, w_in, gate_w1, gate_w2, gate_b, norm_g, w_out, ctx_out):
    def project(u):
        B, L, _ = u.shape
        q, k, v, r = jnp.split(u @ w_in, [GLA_KEY, 2 * GLA_KEY, 2 * GLA_KEY + GLA_VAL], axis=-1)
        z = jnp.einsum('zblr,zrk->zblk', jnp.einsum('bld,zdr->zblr', u, gate_w1), gate_w2) + gate_b[:, None, None, :]
        log_a = jax.nn.log_sigmoid(z.astype(F32)) / GLA_TAU
        log_a = log_a.reshape(2, B, L, GLA_HEADS, GLA_DK).transpose(0, 1, 3, 2, 4)
        return (_heads(q, GLA_HEADS) * GLA_DK ** -0.5, _heads(k, GLA_HEADS), _heads(v, GLA_HEADS), r, log_a)

    def bidir(p, S0f, S0b, with_output):
        q, k, v, _, la = p
        o_f, S_f = _gla_scan(q, k, v, la[0], S0f, with_output)
        o_b, S_b = _gla_scan(_flip(q), _flip(k), _flip(v), _flip(la[1]), S0b, with_output)
        return (o_f + _flip(o_b) if with_output else None), S_f, S_b

    def readout(o, r):
        o = _merge(_rms_norm(o, norm_g)).astype(r.dtype) * jax.nn.silu(r)
        return o @ w_out

    pc = project(u_c)
    S0 = jnp.zeros((u_c.shape[0], GLA_HEADS, GLA_DK, GLA_DV), F32)
    o_c, S_f, S_b = bidir(pc, S0, S0, ctx_out)
    px = project(u_x)
    o_x, _, _ = bidir(px, S_f, S_b, True)
    y_c = readout(o_c, pc[3]) if ctx_out else None
    return y_c, readout(o_x, px[3])


def _delta_scan(q, k, v, beta, g, S0, with_output):
    DV = v.shape[-1]
    q, k, v = (_to_chunks(t.astype(F32)) for t in (q, k, v))
    beta = _to_chunks(beta)
    gc = jnp.cumsum(_to_chunks(g), axis=-1)
    idx = jnp.arange(CHUNK)
    incl = idx[:, None] >= idx[None, :]
    decay_mat = jnp.exp(jnp.where(incl, gc[..., :, None] - gc[..., None, :], -jnp.inf))
    kb = k * beta[..., None]
    a_strict = jnp.where(idx[:, None] > idx[None, :],
                         jnp.einsum('nbhcd,nbhsd->nbhcs', kb, k) * decay_mat, 0.0)
    rhs = jnp.concatenate([v * beta[..., None], kb * jnp.exp(gc)[..., None]], -1)
    sol = lax.linalg.triangular_solve(a_strict, rhs, left_side=True, lower=True, unit_diagonal=True)
    u, w = sol[..., :DV], sol[..., DV:]

    def step(S, inp):
        qc, kc, uc, wc, gcc, dm = inp
        v_new = uc - jnp.einsum('bhcd,bhde->bhce', wc, S)
        g_last = gcc[..., -1]
        S_new = jnp.exp(g_last)[..., None, None] * S + jnp.einsum(
            'bhcd,bhce->bhde', kc * jnp.exp(g_last[..., None] - gcc)[..., None], v_new)
        if not with_output:
            return S_new, None
        o = (jnp.einsum('bhcd,bhde->bhce', qc * jnp.exp(gcc)[..., None], S)
             + jnp.einsum('bhcs,bhse->bhce', jnp.einsum('bhcd,bhsd->bhcs', qc, kc) * dm, v_new))
        return S_new, o

    S_fin, o = lax.scan(step, S0, (q, k, u, w, gc, decay_mat))
    return (_from_chunks(o) if with_output else None), S_fin


def _delta_mixer(u_c, u_x, w_in, conv_w, A_log, dt_bias, norm_g, w_out, ctx_out):
    def project(u):
        B, L, _ = u.shape
        qkv, z, ba = jnp.split(u @ w_in, [2 * DN_KEY + DN_VAL, 2 * DN_KEY + 2 * DN_VAL], axis=-1)
        qkv = jax.nn.silu(_dwconv(qkv, conv_w))
        q, k, v = jnp.split(qkv, [DN_KEY, 2 * DN_KEY], axis=-1)
        q = _l2norm(_heads(q, DN_HEADS)) * DN_DK ** -0.5
        k = _l2norm(_heads(k, DN_HEADS))
        ba = ba.astype(F32).reshape(B, L, 2, 2, DN_HEADS).transpose(2, 3, 0, 4, 1)
        beta = jax.nn.sigmoid(ba[0])
        g = -jnp.exp(A_log.astype(F32))[:, None, :, None] * jax.nn.softplus(
            ba[1] + dt_bias.astype(F32)[:, None, :, None])
        return q, k, _heads(v, DN_HEADS), z, beta, g

    def bidir(p, S0f, S0b, with_output):
        q, k, v, _, beta, g = p
        o_f, S_f = _delta_scan(q, k, v, beta[0], g[0], S0f, with_output)
        o_b, S_b = _delta_scan(_flip(q), _flip(k), _flip(v), _flip(beta[1]), _flip(g[1]), S0b, with_output)
        return (o_f + _flip(o_b) if with_output else None), S_f, S_b

    def readout(o, z):
        o = _rms_norm(o, norm_g) * jax.nn.silu(_heads(z, DN_HEADS).astype(F32))
        return _merge(o).astype(z.dtype) @ w_out

    pc = project(u_c)
    S0 = jnp.zeros((u_c.shape[0], DN_HEADS, DN_DK, DN_DV), F32)
    o_c, S_f, S_b = bidir(pc, S0, S0, ctx_out)
    px = project(u_x)
    o_x, _, _ = bidir(px, S_f, S_b, True)
    y_c = readout(o_c, pc[3]) if ctx_out else None
    return y_c, readout(o_x, px[3])


def _hyena_filter_spectrum(L, pe_w1, pe_b1, freq, pe_w2, pe_b2, pe_w3, decay):
    t = jnp.arange(L, dtype=F32)
    tn = t / L
    ang = (2.0 * math.pi / L) * t[:, None] * jnp.arange(1, HY_BANDS + 1, dtype=F32)[None, :]
    feats = jnp.concatenate([tn[:, None], jnp.cos(ang), jnp.sin(ang)], -1)
    f = freq.astype(F32)
    h = jnp.sin(f * (feats @ pe_w1.astype(F32) + pe_b1.astype(F32)))
    h = jnp.sin(f * (h @ pe_w2.astype(F32) + pe_b2.astype(F32)))
    h = (h @ pe_w3.astype(F32)) * (jnp.exp(-tn[:, None] * decay.astype(F32)) + HY_SHIFT)
    h = h.reshape(L, HY_ORDER, HY_SIDES, D_MODEL)
    h_circ = jnp.concatenate([h[:, :, 0], jnp.zeros((1, HY_ORDER, D_MODEL), F32), h[:0:-1, :, 1]], axis=0)
    return jnp.fft.rfft(h_circ, axis=0)


def _fft_conv(u, spec):
    L = u.shape[1]
    return jnp.fft.irfft(jnp.fft.rfft(u, n=2 * L, axis=1) * spec[None], n=2 * L, axis=1)[:, :L]


def _hyena_seq(u, w_in, conv_w, filt, skip, w_out):
    L = u.shape[1]
    x1, x2, v = jnp.split(_dwconv(u @ w_in, conv_w).astype(F32), 3, axis=-1)
    spec = _hyena_filter_spectrum(L, *filt)
    sk = skip.astype(F32)
    z = x1 * (_fft_conv(v, spec[:, 0]) + sk[0] * v)
    y = x2 * (_fft_conv(z, spec[:, 1]) + sk[1] * z)
    return y.astype(u.dtype) @ w_out


def _swiglu(h, w_gu, w_down):
    gate, up = jnp.split(h @ w_gu, 2, axis=-1)
    return (jax.nn.silu(gate) * up) @ w_down


def _moe(h, router, w_gu, w_down):
    logits = (h @ router).astype(F32)
    top_val, top_idx = lax.top_k(logits, TOP_K)
    weights = jax.nn.softmax(top_val, axis=-1)
    gates = jnp.einsum('...k,...ke->...e', weights, jax.nn.one_hot(top_idx, N_EXPERTS, dtype=F32)).astype(h.dtype)
    out = jnp.zeros_like(h)
    for e in range(N_EXPERTS):
        out = out + gates[..., e:e + 1] * _swiglu(h, w_gu[e], w_down[e])
    return out


def setup_inputs(seed: int = 0) -> dict:
    key = jax.random.key(seed)
    ks = iter(jax.random.split(key, 48))
    D = D_MODEL

    def nrm(shape, scale=1.0):
        return scale * jax.random.normal(next(ks), shape, F32)

    def near_one(shape):
        return 1.0 + nrm(shape, 0.02)

    NG, ND, NH = N_GLA_LAYERS, N_DN_LAYERS, N_HY_LAYERS
    dn_in_width = 2 * DN_KEY + 2 * DN_VAL + 4 * DN_HEADS
    dt = jnp.exp(jax.random.uniform(next(ks), (ND, 2, DN_HEADS), F32, math.log(1e-3), math.log(1e-1)))
    a_init = jax.random.uniform(next(ks), (ND, 2, DN_HEADS), F32, 1.0, 16.0)
    hy_rates = jnp.linspace(HY_DECAY_MIN, HY_DECAY_MAX, 4 * D, dtype=F32)[None, :]
    return {
        'x': nrm((BATCH, SEQ, D)),
        'c': nrm((BATCH, D)),
        'ctx': nrm((BATCH, CTX_LEN, D)),
        'c_ctx': nrm((D,)),
        'ada_w': nrm((DEPTH, D, 6 * D), D ** -0.5),
        'ada_b': nrm((DEPTH, 6 * D), 0.02),
        'ln_g': near_one((DEPTH, 2, D)),
        'ln_b': nrm((DEPTH, 2, D), 0.02),
        'gla_w_in': nrm((NG, D, 2 * GLA_KEY + 2 * GLA_VAL), D ** -0.5),
        'gla_gate_w1': nrm((NG, 2, D, GLA_RANK), D ** -0.5),
        'gla_gate_w2': nrm((NG, 2, GLA_RANK, GLA_KEY), GLA_RANK ** -0.5),
        'gla_gate_b': nrm((NG, 2, GLA_KEY), 0.02),
        'gla_norm_g': near_one((NG, GLA_DV)),
        'gla_w_out': nrm((NG, GLA_VAL, D), GLA_VAL ** -0.5 * DEEPNORM_BETA),
        'dn_w_in': nrm((ND, D, dn_in_width), D ** -0.5),
        'dn_conv': nrm((ND, DN_CONV, 2 * DN_KEY + DN_VAL), DN_CONV ** -0.5),
        'dn_A_log': jnp.log(a_init),
        'dn_dt_bias': dt + jnp.log(-jnp.expm1(-dt)),
        'dn_norm_g': near_one((ND, DN_DV)),
        'dn_w_out': nrm((ND, DN_VAL, D), DN_VAL ** -0.5 * DEEPNORM_BETA),
        'hy_w_in': nrm((NH, D, 3 * D), D ** -0.5),
        'hy_conv': nrm((NH, HY_CONV, 3 * D), HY_CONV ** -0.5),
        'hy_pe_w1': nrm((NH, HY_PE_DIM, HY_FFN), HY_PE_DIM ** -0.5),
        'hy_pe_b1': nrm((NH, HY_FFN), 0.02),
        'hy_freq': near_one((NH, HY_FFN)),
        'hy_pe_w2': nrm((NH, HY_FFN, HY_FFN), HY_FFN ** -0.5),
        'hy_pe_b2': nrm((NH, HY_FFN), 0.02),
        'hy_pe_w3': nrm((NH, HY_FFN, HY_ORDER * HY_SIDES * D), 0.02),
        'hy_decay': hy_rates * (1.0 + nrm((NH, 4 * D), 0.01)),
        'hy_skip': nrm((NH, HY_ORDER, D), 0.1),
        'hy_w_out': nrm((NH, D, D), D ** -0.5 * DEEPNORM_BETA),
        'ffn_w_gu': nrm((N_DENSE_LAYERS, D, 2 * FFN_DENSE), D ** -0.5),
        'ffn_w_down': nrm((N_DENSE_LAYERS, FFN_DENSE, D), FFN_DENSE ** -0.5 * DEEPNORM_BETA),
        'moe_router': nrm((N_MOE_LAYERS, D, N_EXPERTS), D ** -0.5),
        'moe_w_gu': nrm((N_MOE_LAYERS, N_EXPERTS, D, 2 * FFN_EXPERT), D ** -0.5),
        'moe_w_down': nrm((N_MOE_LAYERS, N_EXPERTS, FFN_EXPERT, D), FFN_EXPERT ** -0.5 * DEEPNORM_BETA),
    }


def reference(x, c, ctx, c_ctx, ada_w, ada_b, ln_g, ln_b,
              gla_w_in, gla_gate_w1, gla_gate_w2, gla_gate_b, gla_norm_g, gla_w_out,
              dn_w_in, dn_conv, dn_A_log, dn_dt_bias, dn_norm_g, dn_w_out,
              hy_w_in, hy_conv, hy_pe_w1, hy_pe_b1, hy_freq, hy_pe_w2, hy_pe_b2, hy_pe_w3,
              hy_decay, hy_skip, hy_w_out,
              ffn_w_gu, ffn_w_down, moe_router, moe_w_gu, moe_w_down):
    ROWS = x.shape[1] // GRID_W
    x = x + _grid_sincos(ROWS, x.dtype)[None]
    xc = ctx
    cond_x = jax.nn.silu(c)
    cond_c = jax.nn.silu(c_ctx)
    for i in range(DEPTH):
        last = i == DEPTH - 1
        kind = i % N_MIXERS
        j = i // N_MIXERS
        sh1, sc1, g1, sh2, sc2, g2 = jnp.split((cond_x @ ada_w[i] + ada_b[i])[:, None, :], 6, axis=-1)
        csh1, csc1, cg1, csh2, csc2, cg2 = jnp.split(cond_c @ ada_w[i] + ada_b[i], 6, axis=-1)
        u_x = x * (1 + sc1) + sh1
        u_c = None if (last and kind == 2) else xc * (1 + csc1) + csh1
        if kind == 0:
            y_c, y_x = _gla_mixer(u_c, u_x, gla_w_in[j], gla_gate_w1[j], gla_gate_w2[j], gla_gate_b[j],
                                  gla_norm_g[j], gla_w_out[j], not last)
        elif kind == 1:
            y_c, y_x = _delta_mixer(u_c, u_x, dn_w_in[j], dn_conv[j], dn_A_log[j], dn_dt_bias[j],
                                    dn_norm_g[j], dn_w_out[j], not last)
        else:
            filt = (hy_pe_w1[j], hy_pe_b1[j], hy_freq[j], hy_pe_w2[j], hy_pe_b2[j], hy_pe_w3[j], hy_decay[j])
            y_x = _hyena_seq(u_x, hy_w_in[j], hy_conv[j], filt, hy_skip[j], hy_w_out[j])
            y_c = None if last else _hyena_seq(u_c, hy_w_in[j], hy_conv[j], filt, hy_skip[j], hy_w_out[j])
        x = _layer_norm(DEEPNORM_ALPHA * x + g1 * y_x, ln_g[i, 0], ln_b[i, 0])
        if not last:
            xc = _layer_norm(DEEPNORM_ALPHA * xc + cg1 * y_c, ln_g[i, 0], ln_b[i, 0])

        def channel_mix(h):
            if i % 2 == 0:
                return _swiglu(h, ffn_w_gu[i // 2], ffn_w_down[i // 2])
            return _moe(h, moe_router[i // 2], moe_w_gu[i // 2], moe_w_down[i // 2])

        x = _layer_norm(DEEPNORM_ALPHA * x + g2 * channel_mix(x * (1 + sc2) + sh2), ln_g[i, 1], ln_b[i, 1])
        if not last:
            xc = _layer_norm(DEEPNORM_ALPHA * xc + cg2 * channel_mix(xc * (1 + csc2) + csh2), ln_g[i, 1], ln_b[i, 1])
    return x
```

```python
import functools
import math

import jax
import jax.numpy as jnp
from jax import lax
from jax.experimental import pallas as pl
from jax.experimental.pallas import tpu as pltpu

F32 = jnp.float32
BF16 = jnp.bfloat16

D_MODEL = 1024
DEPTH = 4
GRID_W = 64
LN_EPS = 1e-5
NORM_EPS = 1e-6
DEEPNORM_ALPHA = (2.0 * DEPTH) ** 0.25
CHUNK = 64
N_MIXERS = 3

GLA_HEADS = 4
GLA_DK = D_MODEL // (2 * GLA_HEADS)
GLA_DV = D_MODEL // GLA_HEADS
GLA_KEY = GLA_HEADS * GLA_DK
GLA_VAL = GLA_HEADS * GLA_DV
GLA_TAU = 16.0

DN_HEADS = 8
DN_DK = D_MODEL // DN_HEADS
DN_DV = D_MODEL // DN_HEADS
DN_KEY = DN_HEADS * DN_DK
DN_VAL = DN_HEADS * DN_DV

HY_ORDER = 2
HY_SIDES = 2
HY_BANDS = 16
HY_SHIFT = 0.05

N_EXPERTS = 8
TOP_K = 2

VMEM_LIMIT_BYTES = 56 * 1024 * 1024


def _mm_kernel(a_ref, b_ref, o_ref, acc_ref, *, nk):
    k = pl.program_id(2)

    @pl.when(k == 0)
    def _():
        acc_ref[...] = jnp.zeros_like(acc_ref)

    acc_ref[...] += jnp.dot(a_ref[...].astype(BF16), b_ref[...], preferred_element_type=F32)

    @pl.when(k == nk - 1)
    def _():
        o_ref[...] = acc_ref[...]


def _pick(n, cands):
    for c in cands:
        if n % c == 0:
            return c
    return n


def _mm(a, w):
    M, K = a.shape
    N = w.shape[1]
    n_pad = (-N) % 128
    wb = w.astype(BF16)
    if n_pad:
        wb = jnp.pad(wb, ((0, 0), (0, n_pad)))
    Np = N + n_pad
    tm = _pick(M, (512, 256, 128, 64, 32, 16, 8))
    tn = _pick(Np, (1024, 768, 512, 384, 256, 128))
    tk = _pick(K, (1024, 512, 256, 128))
    nk = K // tk
    out = pl.pallas_call(
        functools.partial(_mm_kernel, nk=nk),
        grid=(M // tm, Np // tn, nk),
        in_specs=[pl.BlockSpec((tm, tk), lambda i, j, k: (i, k)),
                  pl.BlockSpec((tk, tn), lambda i, j, k: (k, j))],
        out_specs=pl.BlockSpec((tm, tn), lambda i, j, k: (i, j)),
        out_shape=jax.ShapeDtypeStruct((M, Np), F32),
        scratch_shapes=[pltpu.VMEM((tm, tn), F32)],
        compiler_params=pltpu.CompilerParams(
            dimension_semantics=("parallel", "parallel", "arbitrary"),
            vmem_limit_bytes=VMEM_LIMIT_BYTES),
        name="mm",
    )(a, wb)
    return out[:, :N] if n_pad else out


def _ffn_kernel(te_ref, u_ref, wg_ref, wu_ref, wd_ref, o_ref, acc_ref, *, nf):
    f = pl.program_id(1)

    @pl.when(f == 0)
    def _():
        acc_ref[...] = jnp.zeros_like(acc_ref)

    u = u_ref[...].astype(BF16)
    g = jnp.dot(u, wg_ref[0], preferred_element_type=F32)
    up = jnp.dot(u, wu_ref[0], preferred_element_type=F32)
    act = (g * jax.nn.sigmoid(g) * up).astype(BF16)
    acc_ref[...] += jnp.dot(act, wd_ref[0], preferred_element_type=F32)

    @pl.when(f == nf - 1)
    def _():
        o_ref[...] = acc_ref[...]


def _ffn(u, w_gu, w_down, tile_expert, tm):
    M, D = u.shape
    F = w_down.shape[1]
    tf = _pick(F, (512, 256, 128))
    nf = F // tf
    grid_spec = pltpu.PrefetchScalarGridSpec(
        num_scalar_prefetch=1,
        grid=(M // tm, nf),
        in_specs=[pl.BlockSpec((tm, D), lambda i, f, te: (i, 0)),
                  pl.BlockSpec((1, D, tf), lambda i, f, te: (te[i], 0, f)),
                  pl.BlockSpec((1, D, tf), lambda i, f, te: (te[i], 0, f + nf)),
                  pl.BlockSpec((1, tf, D), lambda i, f, te: (te[i], f, 0))],
        out_specs=pl.BlockSpec((tm, D), lambda i, f, te: (i, 0)),
        scratch_shapes=[pltpu.VMEM((tm, D), F32)])
    return pl.pallas_call(
        functools.partial(_ffn_kernel, nf=nf),
        grid_spec=grid_spec,
        out_shape=jax.ShapeDtypeStruct((M, D), F32),
        compiler_params=pltpu.CompilerParams(
            dimension_semantics=("parallel", "arbitrary"),
            vmem_limit_bytes=VMEM_LIMIT_BYTES),
        name="ffn",
    )(tile_expert, u, w_gu, w_gu, w_down)


def _mm3(u, w):
    B, L, K = u.shape
    return _mm(u.reshape(B * L, K), w).reshape(B, L, -1)


def _layer_norm(x, g, b):
    xc = x - jnp.mean(x, -1, keepdims=True)
    var = jnp.mean(xc * xc, -1, keepdims=True)
    return xc * lax.rsqrt(var + LN_EPS) * g + b


def _rms_norm(x, g):
    return x * lax.rsqrt(jnp.mean(x * x, -1, keepdims=True) + NORM_EPS) * g


def _l2norm(x):
    return x * lax.rsqrt(jnp.sum(x * x, -1, keepdims=True) + NORM_EPS)


def _heads(t, h):
    B, L, _ = t.shape
    return t.reshape(B, L, h, -1).transpose(0, 2, 1, 3)


def _merge(t):
    B, H, L, d = t.shape
    return t.transpose(0, 2, 1, 3).reshape(B, L, H * d)


def _flip(t):
    return jnp.flip(t, axis=2)


def _dwconv(x, w):
    K, C = w.shape
    return lax.conv_general_dilated(x, w[:, None, :].astype(x.dtype), window_strides=(1,),
                                    padding=[((K - 1) // 2, K // 2)],
                                    dimension_numbers=('NWC', 'WIO', 'NWC'),
                                    feature_group_count=C)


def _to_chunks(t):
    B, H, L = t.shape[:3]
    t = t.reshape(B, H, L // CHUNK, CHUNK, *t.shape[3:])
    return jnp.moveaxis(t, 2, 0)


def _from_chunks(t):
    N, B, H, C = t.shape[:4]
    return jnp.moveaxis(t, 0, 2).reshape(B, H, N * C, *t.shape[4:])


def _grid_sincos(rows, dtype):
    r, col = jnp.meshgrid(jnp.arange(rows, dtype=F32), jnp.arange(GRID_W, dtype=F32), indexing='ij')
    n_freq = D_MODEL // 4
    omega = 1.0 / (10000.0 ** (jnp.arange(n_freq, dtype=F32) / n_freq))

    def emb(p):
        ang = p.reshape(-1)[:, None] * omega[None, :]
        return jnp.concatenate([jnp.sin(ang), jnp.cos(ang)], -1)

    return jnp.concatenate([emb(r), emb(col)], -1).astype(dtype)


def _gla_scan(q, k, v, log_a, S0, with_output):
    q, k, v = (_to_chunks(t.astype(F32)) for t in (q, k, v))
    b = jnp.cumsum(_to_chunks(log_a), axis=3)
    tril = jnp.tril(jnp.ones((CHUNK, CHUNK), dtype=bool))

    def step(S, inp):
        qc, kc, vc, bc = inp
        b_last = bc[:, :, -1:, :]
        S_new = jnp.exp(b_last[:, :, 0, :, None]) * S + jnp.einsum(
            'bhcd,bhce->bhde', kc * jnp.exp(b_last - bc), vc)
        if not with_output:
            return S_new, None
        q_dec = qc * jnp.exp(bc)
        scores = jnp.where(tril, jnp.einsum('bhcd,bhsd->bhcs', q_dec, kc * jnp.exp(-bc)), 0.0)
        o = jnp.einsum('bhcs,bhse->bhce', scores, vc) + jnp.einsum('bhcd,bhde->bhce', q_dec, S)
        return S_new, o

    S_fin, o = lax.scan(step, S0, (q, k, v, b))
    return (_from_chunks(o) if with_output else None), S_fin


def _gla_mixer(u_c, u_x, w_in, gate_w1, gate_w2, gate_b, norm_g, w_out, ctx_out):
    def project(u):
        B, L, _ = u.shape
        q, k, v, r = jnp.split(_mm3(u, w_in), [GLA_KEY, 2 * GLA_KEY, 2 * GLA_KEY + GLA_VAL], axis=-1)
        z = jnp.einsum('zblr,zrk->zblk', jnp.einsum('bld,zdr->zblr', u, gate_w1), gate_w2) + gate_b[:, None, None, :]
        log_a = jax.nn.log_sigmoid(z.astype(F32)) / GLA_TAU
        log_a = log_a.reshape(2, B, L, GLA_HEADS, GLA_DK).transpose(0, 1, 3, 2, 4)
        return (_heads(q, GLA_HEADS) * GLA_DK ** -0.5, _heads(k, GLA_HEADS), _heads(v, GLA_HEADS), r, log_a)

    def bidir(p, S0f, S0b, with_output):
        q, k, v, _, la = p
        o_f, S_f = _gla_scan(q, k, v, la[0], S0f, with_output)
        o_b, S_b = _gla_scan(_flip(q), _flip(k), _flip(v), _flip(la[1]), S0b, with_output)
        return (o_f + _flip(o_b) if with_output else None), S_f, S_b

    def readout(o, r):
        o = _merge(_rms_norm(o, norm_g)) * jax.nn.silu(r)
        return _mm3(o, w_out)

    pc = project(u_c)
    S0 = jnp.zeros((u_c.shape[0], GLA_HEADS, GLA_DK, GLA_DV), F32)
    o_c, S_f, S_b = bidir(pc, S0, S0, ctx_out)
    px = project(u_x)
    o_x, _, _ = bidir(px, S_f, S_b, True)
    y_c = readout(o_c, pc[3]) if ctx_out else None
    return y_c, readout(o_x, px[3])


def _delta_scan(q, k, v, beta, g, S0, with_output):
    DV = v.shape[-1]
    q, k, v = (_to_chunks(t.astype(F32)) for t in (q, k, v))
    beta = _to_chunks(beta)
    gc = jnp.cumsum(_to_chunks(g), axis=-1)
    idx = jnp.arange(CHUNK)
    incl = idx[:, None] >= idx[None, :]
    decay_mat = jnp.exp(jnp.where(incl, gc[..., :, None] - gc[..., None, :], -jnp.inf))
    kb = k * beta[..., None]
    a_strict = jnp.where(idx[:, None] > idx[None, :],
                         jnp.einsum('nbhcd,nbhsd->nbhcs', kb, k) * decay_mat, 0.0)
    rhs = jnp.concatenate([v * beta[..., None], kb * jnp.exp(gc)[..., None]], -1)
    sol = lax.linalg.triangular_solve(a_strict, rhs, left_side=True, lower=True, unit_diagonal=True)
    u, w = sol[..., :DV], sol[..., DV:]

    def step(S, inp):
        qc, kc, uc, wc, gcc, dm = inp
        v_new = uc - jnp.einsum('bhcd,bhde->bhce', wc, S)
        g_last = gcc[..., -1]
        S_new = jnp.exp(g_last)[..., None, None] * S + jnp.einsum(
            'bhcd,bhce->bhde', kc * jnp.exp(g_last[..., None] - gcc)[..., None], v_new)
        if not with_output:
            return S_new, None
        o = (jnp.einsum('bhcd,bhde->bhce', qc * jnp.exp(gcc)[..., None], S)
             + jnp.einsum('bhcs,bhse->bhce', jnp.einsum('bhcd,bhsd->bhcs', qc, kc) * dm, v_new))
        return S_new, o

    S_fin, o = lax.scan(step, S0, (q, k, u, w, gc, decay_mat))
    return (_from_chunks(o) if with_output else None), S_fin


def _delta_mixer(u_c, u_x, w_in, conv_w, A_log, dt_bias, norm_g, w_out, ctx_out):
    def project(u):
        B, L, _ = u.shape
        qkv, z, ba = jnp.split(_mm3(u, w_in), [2 * DN_KEY + DN_VAL, 2 * DN_KEY + 2 * DN_VAL], axis=-1)
        qkv = jax.nn.silu(_dwconv(qkv, conv_w))
        q, k, v = jnp.split(qkv, [DN_KEY, 2 * DN_KEY], axis=-1)
        q = _l2norm(_heads(q, DN_HEADS)) * DN_DK ** -0.5
        k = _l2norm(_heads(k, DN_HEADS))
        ba = ba.reshape(B, L, 2, 2, DN_HEADS).transpose(2, 3, 0, 4, 1)
        beta = jax.nn.sigmoid(ba[0])
        g = -jnp.exp(A_log)[:, None, :, None] * jax.nn.softplus(ba[1] + dt_bias[:, None, :, None])
        return q, k, _heads(v, DN_HEADS), z, beta, g

    def bidir(p, S0f, S0b, with_output):
        q, k, v, _, beta, g = p
        o_f, S_f = _delta_scan(q, k, v, beta[0], g[0], S0f, with_output)
        o_b, S_b = _delta_scan(_flip(q), _flip(k), _flip(v), _flip(beta[1]), _flip(g[1]), S0b, with_output)
        return (o_f + _flip(o_b) if with_output else None), S_f, S_b

    def readout(o, z):
        o = _rms_norm(o, norm_g) * jax.nn.silu(_heads(z, DN_HEADS))
        return _mm3(_merge(o), w_out)

    pc = project(u_c)
    S0 = jnp.zeros((u_c.shape[0], DN_HEADS, DN_DK, DN_DV), F32)
    o_c, S_f, S_b = bidir(pc, S0, S0, ctx_out)
    px = project(u_x)
    o_x, _, _ = bidir(px, S_f, S_b, True)
    y_c = readout(o_c, pc[3]) if ctx_out else None
    return y_c, readout(o_x, px[3])


def _hyena_filter_spectrum(L, pe_w1, pe_b1, freq, pe_w2, pe_b2, pe_w3, decay):
    t = jnp.arange(L, dtype=F32)
    tn = t / L
    ang = (2.0 * math.pi / L) * t[:, None] * jnp.arange(1, HY_BANDS + 1, dtype=F32)[None, :]
    feats = jnp.concatenate([tn[:, None], jnp.cos(ang), jnp.sin(ang)], -1)
    h = jnp.sin(freq * (feats @ pe_w1 + pe_b1))
    h = jnp.sin(freq * (h @ pe_w2 + pe_b2))
    h = (h @ pe_w3) * (jnp.exp(-tn[:, None] * decay) + HY_SHIFT)
    h = h.reshape(L, HY_ORDER, HY_SIDES, D_MODEL)
    h_circ = jnp.concatenate([h[:, :, 0], jnp.zeros((1, HY_ORDER, D_MODEL), F32), h[:0:-1, :, 1]], axis=0)
    return jnp.fft.rfft(h_circ, axis=0)


def _fft_conv(u, spec):
    L = u.shape[1]
    return jnp.fft.irfft(jnp.fft.rfft(u, n=2 * L, axis=1) * spec[None], n=2 * L, axis=1)[:, :L]


def _hyena_seq(u, w_in, conv_w, filt, skip, w_out):
    L = u.shape[1]
    x1, x2, v = jnp.split(_dwconv(_mm3(u, w_in), conv_w), 3, axis=-1)
    spec = _hyena_filter_spectrum(L, *filt)
    z = x1 * (_fft_conv(v, spec[:, 0]) + skip[0] * v)
    y = x2 * (_fft_conv(z, spec[:, 1]) + skip[1] * z)
    return _mm3(y, w_out)


def _swiglu(h, w_gu, w_down):
    B, L, D = h.shape
    M = B * L
    tm = _pick(M, (512, 256, 128))
    te = jnp.zeros((M // tm,), jnp.int32)
    return _ffn(h.reshape(M, D), w_gu.astype(BF16)[None], w_down.astype(BF16)[None], te, tm).reshape(B, L, D)


def _moe(h, router, w_gu, w_down):
    B, L, D = h.shape
    M = B * L
    logits = jnp.einsum('bld,de->ble', h, router, precision=lax.Precision.HIGHEST)
    top_val, top_idx = lax.top_k(logits, TOP_K)
    weights = jax.nn.softmax(top_val, axis=-1)
    gates = jnp.einsum('...k,...ke->...e', weights, jax.nn.one_hot(top_idx, N_EXPERTS, dtype=F32))
    tm = _pick(M, (512, 256, 128))
    wgu = w_gu.astype(BF16)
    wd = w_down.astype(BF16)
    out = jnp.zeros_like(h)
    hf = h.reshape(M, D)
    for e in range(N_EXPERTS):
        te = jnp.full((M // tm,), e, jnp.int32)
        out = out + gates[..., e:e + 1] * _ffn(hf, wgu, wd, te, tm).reshape(B, L, D)
    return out


def kernel(x, c, ctx, c_ctx, ada_w, ada_b, ln_g, ln_b,
           gla_w_in, gla_gate_w1, gla_gate_w2, gla_gate_b, gla_norm_g, gla_w_out,
           dn_w_in, dn_conv, dn_A_log, dn_dt_bias, dn_norm_g, dn_w_out,
           hy_w_in, hy_conv, hy_pe_w1, hy_pe_b1, hy_freq, hy_pe_w2, hy_pe_b2, hy_pe_w3,
           hy_decay, hy_skip, hy_w_out,
           ffn_w_gu, ffn_w_down, moe_router, moe_w_gu, moe_w_down):
    ROWS = x.shape[1] // GRID_W
    x = x + _grid_sincos(ROWS, x.dtype)[None]
    xc = ctx
    cond_x = jax.nn.silu(c)
    cond_c = jax.nn.silu(c_ctx)
    for i in range(DEPTH):
        last = i == DEPTH - 1
        kind = i % N_MIXERS
        j = i // N_MIXERS
        mod_x = jnp.dot(cond_x, ada_w[i], precision=lax.Precision.HIGHEST) + ada_b[i]
        mod_c = jnp.dot(cond_c, ada_w[i], precision=lax.Precision.HIGHEST) + ada_b[i]
        sh1, sc1, g1, sh2, sc2, g2 = jnp.split(mod_x[:, None, :], 6, axis=-1)
        csh1, csc1, cg1, csh2, csc2, cg2 = jnp.split(mod_c, 6, axis=-1)
        u_x = x * (1 + sc1) + sh1
        u_c = None if (last and kind == 2) else xc * (1 + csc1) + csh1
        if kind == 0:
            y_c, y_x = _gla_mixer(u_c, u_x, gla_w_in[j], gla_gate_w1[j], gla_gate_w2[j], gla_gate_b[j],
                                  gla_norm_g[j], gla_w_out[j], not last)
        elif kind == 1:
            y_c, y_x = _delta_mixer(u_c, u_x, dn_w_in[j], dn_conv[j], dn_A_log[j], dn_dt_bias[j],
                                    dn_norm_g[j], dn_w_out[j], not last)
        else:
            filt = (hy_pe_w1[j], hy_pe_b1[j], hy_freq[j], hy_pe_w2[j], hy_pe_b2[j], hy_pe_w3[j], hy_decay[j])
            y_x = _hyena_seq(u_x, hy_w_in[j], hy_conv[j], filt, hy_skip[j], hy_w_out[j])
            y_c = None if last else _hyena_seq(u_c, hy_w_in[j], hy_conv[j], filt, hy_skip[j], hy_w_out[j])
        x = _layer_norm(DEEPNORM_ALPHA * x + g1 * y_x, ln_g[i, 0], ln_b[i, 0])
        if not last:
            xc = _layer_norm(DEEPNORM_ALPHA * xc + cg1 * y_c, ln_g[i, 0], ln_b[i, 0])

        def channel_mix(h):
            if i % 2 == 0:
                return _swiglu(h, ffn_w_gu[i // 2], ffn_w_down[i // 2])
            return _moe(h, moe_router[i // 2], moe_w_gu[i // 2], moe_w_down[i // 2])

        x = _layer_norm(DEEPNORM_ALPHA * x + g2 * channel_mix(x * (1 + sc2) + sh2), ln_g[i, 1], ln_b[i, 1])
        if not last:
            xc = _layer_norm(DEEPNORM_ALPHA * xc + cg2 * channel_mix(xc * (1 + csc2) + csh2), ln_g[i, 1], ln_b[i, 1])
    return x
```

```python
import functools
import math

import jax
import jax.numpy as jnp
from jax import lax
from jax.experimental import pallas as pl
from jax.experimental.pallas import tpu as pltpu

F32 = jnp.float32
BF16 = jnp.bfloat16
HIGHEST = lax.Precision.HIGHEST

D_MODEL = 1024
DEPTH = 4
GRID_W = 64
LN_EPS = 1e-5
NORM_EPS = 1e-6
DEEPNORM_ALPHA = (2.0 * DEPTH) ** 0.25
CHUNK = 64
N_MIXERS = 3
LANES = 128

GLA_HEADS = 4
GLA_DK = D_MODEL // (2 * GLA_HEADS)
GLA_DV = D_MODEL // GLA_HEADS
GLA_KEY = GLA_HEADS * GLA_DK
GLA_VAL = GLA_HEADS * GLA_DV
GLA_RANK = 16
GLA_TAU = 16.0

DN_HEADS = 8
DN_DK = D_MODEL // DN_HEADS
DN_DV = D_MODEL // DN_HEADS
DN_KEY = DN_HEADS * DN_DK
DN_VAL = DN_HEADS * DN_DV

HY_ORDER = 2
HY_SIDES = 2
HY_BANDS = 16
HY_SHIFT = 0.05

N_EXPERTS = 8
TOP_K = 2
MOE_TILE = 512

VMEM_LIMIT_BYTES = 56 * 1024 * 1024

_NT = (((1,), (1,)), ((), ()))
_TN = (((0,), (0,)), ((), ()))


def _cparams(sem):
    return pltpu.CompilerParams(dimension_semantics=sem, vmem_limit_bytes=VMEM_LIMIT_BYTES)


def _mm_kernel(a_ref, b_ref, o_ref, acc_ref, *, nk):
    k = pl.program_id(2)

    @pl.when(k == 0)
    def _():
        acc_ref[...] = jnp.zeros_like(acc_ref)

    acc_ref[...] += jnp.dot(a_ref[...].astype(BF16), b_ref[...], preferred_element_type=F32)

    @pl.when(k == nk - 1)
    def _():
        o_ref[...] = acc_ref[...]


def _pick(n, cands):
    for c in cands:
        if n % c == 0:
            return c
    return n


def _lane_tile(n, cap):
    best = LANES
    for t in range(LANES, cap + 1, LANES):
        if n % t == 0:
            best = t
    return best


def _mm(a, w):
    M, K = a.shape
    N = w.shape[1]
    n_pad = (-N) % LANES
    wb = w.astype(BF16)
    if n_pad:
        wb = jnp.pad(wb, ((0, 0), (0, n_pad)))
    Np = N + n_pad
    tm = _pick(M, (512, 256, 128, 64, 32, 16, 8))
    tn = _lane_tile(Np, 1408)
    tk = _pick(K, (1024, 512, 256, 128))
    nk = K // tk
    return pl.pallas_call(
        functools.partial(_mm_kernel, nk=nk),
        grid=(M // tm, Np // tn, nk),
        in_specs=[pl.BlockSpec((tm, tk), lambda i, j, k: (i, k)),
                  pl.BlockSpec((tk, tn), lambda i, j, k: (k, j))],
        out_specs=pl.BlockSpec((tm, tn), lambda i, j, k: (i, j)),
        out_shape=jax.ShapeDtypeStruct((M, Np), F32),
        scratch_shapes=[pltpu.VMEM((tm, tn), F32)],
        compiler_params=_cparams(("parallel", "parallel", "arbitrary")),
        name="mm",
    )(a, wb)


def _mm3(u, w):
    B, L, K = u.shape
    return _mm(u.reshape(B * L, K), w).reshape(B, L, -1)


def _ffn_kernel(te_ref, u_ref, wg_ref, wu_ref, wd_ref, o_ref, acc_ref, *, nf):
    f = pl.program_id(1)

    @pl.when(f == 0)
    def _():
        acc_ref[...] = jnp.zeros_like(acc_ref)

    u = u_ref[...].astype(BF16)
    g = jnp.dot(u, wg_ref[0], preferred_element_type=F32)
    up = jnp.dot(u, wu_ref[0], preferred_element_type=F32)
    act = (g * jax.nn.sigmoid(g) * up).astype(BF16)
    acc_ref[...] += jnp.dot(act, wd_ref[0], preferred_element_type=F32)

    @pl.when(f == nf - 1)
    def _():
        o_ref[...] = acc_ref[...]


def _ffn(u, w_gu, w_down, tile_expert, tm):
    M, D = u.shape
    F = w_down.shape[1]
    tf = _pick(F, (512, 256, 128))
    nf = F // tf
    grid_spec = pltpu.PrefetchScalarGridSpec(
        num_scalar_prefetch=1,
        grid=(M // tm, nf),
        in_specs=[pl.BlockSpec((tm, D), lambda i, f, te: (i, 0)),
                  pl.BlockSpec((1, D, tf), lambda i, f, te: (te[i], 0, f)),
                  pl.BlockSpec((1, D, tf), lambda i, f, te: (te[i], 0, f + nf)),
                  pl.BlockSpec((1, tf, D), lambda i, f, te: (te[i], f, 0))],
        out_specs=pl.BlockSpec((tm, D), lambda i, f, te: (i, 0)),
        scratch_shapes=[pltpu.VMEM((tm, D), F32)])
    return pl.pallas_call(
        functools.partial(_ffn_kernel, nf=nf),
        grid_spec=grid_spec,
        out_shape=jax.ShapeDtypeStruct((M, D), F32),
        compiler_params=_cparams(("parallel", "arbitrary")),
        name="ffn",
    )(tile_expert, u, w_gu, w_gu, w_down)


def _layer_norm(x, g, b):
    xc = x - jnp.mean(x, -1, keepdims=True)
    var = jnp.mean(xc * xc, -1, keepdims=True)
    return xc * lax.rsqrt(var + LN_EPS) * g + b


def _head_rms_norm(o, g, heads):
    B, L, W = o.shape
    oh = o.reshape(B, L, heads, W // heads)
    oh = oh * lax.rsqrt(jnp.mean(oh * oh, -1, keepdims=True) + NORM_EPS) * g
    return oh.reshape(B, L, W)


def _dwconv(x, w):
    K, C = w.shape
    return lax.conv_general_dilated(x, w[:, None, :].astype(x.dtype), window_strides=(1,),
                                    padding=[((K - 1) // 2, K // 2)],
                                    dimension_numbers=('NWC', 'WIO', 'NWC'),
                                    feature_group_count=C)


def _grid_sincos(rows, dtype):
    r, col = jnp.meshgrid(jnp.arange(rows, dtype=F32), jnp.arange(GRID_W, dtype=F32), indexing='ij')
    n_freq = D_MODEL // 4
    omega = 1.0 / (10000.0 ** (jnp.arange(n_freq, dtype=F32) / n_freq))

    def emb(p):
        ang = p.reshape(-1)[:, None] * omega[None, :]
        return jnp.concatenate([jnp.sin(ang), jnp.cos(ang)], -1)

    return jnp.concatenate([emb(r), emb(col)], -1).astype(dtype)


def _chunk_masks(rev):
    row = lax.broadcasted_iota(jnp.int32, (CHUNK, CHUNK), 0)
    col = lax.broadcasted_iota(jnp.int32, (CHUNK, CHUNK), 1)
    if rev:
        return row, col, col >= row, col > row
    return row, col, col <= row, col < row


def _bdot(a, b, dims=None):
    a = a.astype(BF16)
    b = b.astype(BF16)
    if dims is None:
        return jnp.dot(a, b, preferred_element_type=F32)
    return lax.dot_general(a, b, dims, preferred_element_type=F32)


def _scan_specs(nc, rev, widths_and_blocks):
    def cidx(j):
        return nc - 1 - j if rev else j
    return [pl.BlockSpec((1, CHUNK, w), functools.partial(lambda b, j, cb: (b, cidx(j), cb), cb=cb))
            for w, cb in widths_and_blocks]


def _gla_kernel(*refs, rev, nc, with_prev):
    if with_prev:
        q_ref, k_ref, v_ref, t_ref, w2_ref, gb_ref, s0_ref, prev_ref, o_ref, sfin_ref, s_scr = refs
    else:
        q_ref, k_ref, v_ref, t_ref, w2_ref, gb_ref, s0_ref, o_ref, sfin_ref, s_scr = refs
        prev_ref = None
    j = pl.program_id(1)

    @pl.when(j == 0)
    def _():
        s_scr[...] = s0_ref[0]

    _, _, incl, _ = _chunk_masks(rev)
    z = _bdot(t_ref[0], w2_ref[...]) + gb_ref[...]
    log_a = (jnp.minimum(z, 0.0) - jnp.log(1.0 + jnp.exp(-jnp.abs(z)))) * (1.0 / GLA_TAU)
    b = jnp.dot(incl.astype(F32), log_a, precision=HIGHEST, preferred_element_type=F32)
    last = 0 if rev else CHUNK - 1
    for h in range(GLA_HEADS):
        ks = slice(h * GLA_DK, (h + 1) * GLA_DK)
        vs = slice(h * GLA_DV, (h + 1) * GLA_DV)
        bh = b[:, ks]
        b_last = bh[last:last + 1, :]
        qf = q_ref[0, :, ks]
        kf = k_ref[0, :, ks]
        vb = v_ref[0, :, vs].astype(BF16)
        q_dec = (qf * (GLA_DK ** -0.5) * jnp.exp(bh)).astype(BF16)
        k_inv = (kf * jnp.exp(-bh)).astype(BF16)
        scores = jnp.where(incl, _bdot(q_dec, k_inv, _NT), 0.0)
        st = s_scr[h]
        o_h = _bdot(scores, vb) + _bdot(q_dec, st, _NT)
        k_dec = kf * jnp.exp(b_last - bh)
        s_scr[h] = st * jnp.exp(b_last) + _bdot(vb, k_dec, _TN)
        if prev_ref is not None:
            o_h = o_h + prev_ref[0, :, vs]
        o_ref[0, :, vs] = o_h

    @pl.when(j == nc - 1)
    def _():
        sfin_ref[0] = s_scr[...]


def _gla_scan(proj, w2, gb, s0, prev, rev):
    B, L, _ = proj.shape
    nc = L // CHUNK
    with_prev = prev is not None
    gate_cb = (2 * GLA_KEY + 2 * GLA_VAL) // LANES
    specs = _scan_specs(nc, rev, [(GLA_KEY, 0), (GLA_KEY, 1), (GLA_VAL, 1), (LANES, gate_cb)])
    specs += [pl.BlockSpec((LANES, GLA_KEY), lambda b, j: (0, 0)),
              pl.BlockSpec((1, GLA_KEY), lambda b, j: (0, 0)),
              pl.BlockSpec((1, GLA_HEADS, GLA_DV, GLA_DK), lambda b, j: (b, 0, 0, 0))]
    args = [proj, proj, proj, proj, w2, gb, s0]
    o_spec = _scan_specs(nc, rev, [(GLA_VAL, 0)])[0]
    if with_prev:
        specs.append(o_spec)
        args.append(prev)
    return pl.pallas_call(
        functools.partial(_gla_kernel, rev=rev, nc=nc, with_prev=with_prev),
        grid=(B, nc),
        in_specs=specs,
        out_specs=[o_spec, pl.BlockSpec((1, GLA_HEADS, GLA_DV, GLA_DK), lambda b, j: (b, 0, 0, 0))],
        out_shape=[jax.ShapeDtypeStruct((B, L, GLA_VAL), F32),
                   jax.ShapeDtypeStruct((B, GLA_HEADS, GLA_DV, GLA_DK), F32)],
        scratch_shapes=[pltpu.VMEM((GLA_HEADS, GLA_DV, GLA_DK), F32)],
        compiler_params=_cparams(("parallel", "arbitrary")),
        name="gla_scan_rev" if rev else "gla_scan_fwd",
    )(*args)


def _gla_mixer(u_c, u_x, w_in, gate_w1, gate_w2, gate_b, norm_g, w_out, ctx_out):
    w_ext = jnp.concatenate([w_in, gate_w1[0], gate_w1[1]], axis=1)
    r_cols = slice(2 * GLA_KEY + GLA_VAL, 2 * GLA_KEY + 2 * GLA_VAL)
    w2 = []
    for d in range(2):
        w2.append(jnp.zeros((LANES, GLA_KEY), F32).at[d * GLA_RANK:(d + 1) * GLA_RANK].set(gate_w2[d]).astype(BF16))
    gb = gate_b[:, None, :]

    def bidir(proj, s0f, s0b):
        o, s_f = _gla_scan(proj, w2[0], gb[0], s0f, None, False)
        o, s_b = _gla_scan(proj, w2[1], gb[1], s0b, o, True)
        return o, s_f, s_b

    def readout(o, proj):
        y = _head_rms_norm(o, norm_g, GLA_HEADS) * jax.nn.silu(proj[..., r_cols])
        return _mm3(y, w_out)

    pc = _mm3(u_c, w_ext)
    s0 = jnp.zeros((u_c.shape[0], GLA_HEADS, GLA_DV, GLA_DK), F32)
    o_c, s_f, s_b = bidir(pc, s0, s0)
    px = _mm3(u_x, w_ext)
    o_x, _, _ = bidir(px, s_f, s_b)
    y_c = readout(o_c, pc) if ctx_out else None
    return y_c, readout(o_x, px)


def _dn_prep_kernel(x_ref, w_ref, o_ref, *, L):
    cb = pl.program_id(1)
    x = x_ref[0]
    w = w_ref[...]
    rowi = lax.broadcasted_iota(jnp.int32, x.shape, 0)
    x_prev = jnp.where(rowi == 0, 0.0, pltpu.roll(x, 1, 0))
    x_next = jnp.where(rowi == L - 1, 0.0, pltpu.roll(x, L - 1, 0))
    y = w[0:1] * x_prev + w[1:2] * x + w[2:3] * x_next
    y = y * jax.nn.sigmoid(y)
    inv = lax.rsqrt(jnp.sum(y * y, axis=1, keepdims=True) + NORM_EPS)
    q_scale = jnp.where(cb < DN_HEADS, DN_DK ** -0.5, 1.0)
    fac = jnp.where(cb < 2 * DN_HEADS, inv * q_scale, 1.0)
    o_ref[0] = (y * fac).astype(BF16)


def _dn_prep(proj, conv_w):
    B, L, _ = proj.shape
    ncb = (2 * DN_KEY + DN_VAL) // LANES
    return pl.pallas_call(
        functools.partial(_dn_prep_kernel, L=L),
        grid=(B, ncb),
        in_specs=[pl.BlockSpec((1, L, LANES), lambda b, c: (b, 0, c)),
                  pl.BlockSpec((conv_w.shape[0], LANES), lambda b, c: (0, c))],
        out_specs=pl.BlockSpec((1, L, LANES), lambda b, c: (b, 0, c)),
        out_shape=jax.ShapeDtypeStruct((B, L, ncb * LANES), BF16),
        compiler_params=_cparams(("parallel", "parallel")),
        name="dn_prep",
    )(proj, conv_w)


DN_GROUP = 4
DN_ROWS = DN_GROUP * CHUNK


def _dn_chunk_kernel(q_ref, k_ref, v_ref, ba_ref, par_ref, p_ref, qm_ref, g_ref, r_ref, oi_ref, *, rev):
    ba = ba_ref[0]
    beta_all = jax.nn.sigmoid(ba)
    sp_in = ba + par_ref[1:2, :]
    softplus = jnp.maximum(sp_in, 0.0) + jnp.log(1.0 + jnp.exp(-jnp.abs(sp_in)))
    g_all = par_ref[0:1, :] * softplus
    prow = lax.broadcasted_iota(jnp.int32, (2 * CHUNK, CHUNK), 0)
    pcol = lax.broadcasted_iota(jnp.int32, (2 * CHUNK, CHUNK), 1)
    tri_pad = (((pcol >= prow) if rev else (pcol <= prow)) & (prow < CHUNK)).astype(F32)
    gc_pad = jnp.dot(tri_pad, g_all, precision=HIGHEST, preferred_element_type=F32)
    gc = gc_pad[:CHUNK]
    gc_t = gc_pad.T
    d = 1 if rev else 0
    last = 0 if rev else CHUNK - 1
    row = lax.broadcasted_iota(jnp.int32, (DN_ROWS, DN_ROWS), 0)
    col = lax.broadcasted_iota(jnp.int32, (DN_ROWS, DN_ROWS), 1)
    same_head = jnp.right_shift(row, 6) == jnp.right_shift(col, 6)
    rt = jnp.bitwise_and(row, CHUNK - 1)
    ct = jnp.bitwise_and(col, CHUNK - 1)
    incl = same_head & ((ct >= rt) if rev else (ct <= rt))
    strict = same_head & ((ct > rt) if rev else (ct < rt))
    groups = []
    for grp in range(DN_HEADS // DN_GROUP):
        heads = range(grp * DN_GROUP, (grp + 1) * DN_GROUP)
        gls = [2 * DN_HEADS + d * DN_HEADS + h for h in heads]
        qs = jnp.concatenate([q_ref[0, :, h * DN_DK:(h + 1) * DN_DK] for h in heads], axis=0)
        ks = jnp.concatenate([k_ref[0, :, h * DN_DK:(h + 1) * DN_DK] for h in heads], axis=0)
        vs = jnp.concatenate([v_ref[0, :, h * DN_DV:(h + 1) * DN_DV] for h in heads], axis=0)
        beta = jnp.concatenate([beta_all[:, d * DN_HEADS + h:d * DN_HEADS + h + 1] for h in heads], axis=0)
        gcol = jnp.concatenate([gc[:, gl:gl + 1] for gl in gls], axis=0)
        glast = jnp.concatenate([jnp.broadcast_to(gc[last:last + 1, gl:gl + 1], (CHUNK, 1)) for gl in gls], axis=0)
        rows = [gc_t[gl:gl + 1, :] for gl in gls]
        grow = jnp.concatenate([rows[0] + pltpu.roll(rows[1], CHUNK, 1),
                                rows[2] + pltpu.roll(rows[3], CHUNK, 1)], axis=1)
        groups.append((heads, gls, qs, ks, vs, beta, gcol, glast, grow))
    dms = [jnp.where(incl, jnp.exp(g[6] - g[8]), 0.0) for g in groups]
    kfs = [g[3].astype(F32) for g in groups]
    kbetas = [kf * g[5] for kf, g in zip(kfs, groups)]
    a_s = [jnp.where(strict, _bdot(kb, g[3], _NT) * dm, 0.0) for kb, g, dm in zip(kbetas, groups, dms)]
    eye = (row == col).astype(F32)
    ts = [eye, eye]
    s = 1
    while s < CHUNK:
        sh = s.bit_length() - 1
        same = jnp.right_shift(row, sh + 1) == jnp.right_shift(col, sh + 1)
        r_hi = jnp.bitwise_and(jnp.right_shift(row, sh), 1) == 1
        c_hi = jnp.bitwise_and(jnp.right_shift(col, sh), 1) == 1
        sel = same & ((c_hi & ~r_hi) if rev else (r_hi & ~c_hi))
        ms = [jnp.where(sel, a, 0.0) for a in a_s]
        if s == 1:
            ts = [eye - m for m in ms]
        else:
            mts = [_bdot(m, t) for m, t in zip(ms, ts)]
            ts = [t - _bdot(t, mt) for t, mt in zip(ts, mts)]
        s *= 2
    rhs = [jnp.concatenate([g[4].astype(F32) * g[5], kb * jnp.exp(g[6])], axis=1) for g, kb in zip(groups, kbetas)]
    sols = [_bdot(t, r) for t, r in zip(ts, rhs)]
    qks = [_bdot(g[2], g[3], _NT) * dm for g, dm in zip(groups, dms)]
    qsols = [_bdot(qk, sol) for qk, sol in zip(qks, sols)]
    for g, kf, sol, qsol in zip(groups, kfs, sols, qsols):
        heads, gls, qs, _, _, _, gcol, glast, _ = g
        k_dec = kf * jnp.exp(glast - gcol)
        r_all = qs.astype(F32) * jnp.exp(gcol) - qsol[:, DN_DV:]
        for i, h in enumerate(heads):
            rs = slice(i * CHUNK, (i + 1) * CHUNK)
            hs = slice(h * DN_DV, (h + 1) * DN_DV)
            pq = _bdot(k_dec[rs], sol[rs], _TN)
            qm_ref[0, 0, h] = pq[:, :DN_DV]
            p_ref[0, 0, h] = (-pq[:, DN_DV:]).astype(BF16)
            g_ref[0, 0, h:h + 1, :] = jnp.broadcast_to(jnp.exp(gc[last:last + 1, gls[i]:gls[i] + 1]), (1, DN_DV))
            r_ref[0, :, hs] = r_all[rs].astype(BF16)
            oi_ref[0, :, hs] = qsol[rs, :DN_DV]


def _dn_chunk(qkv, proj, par, rev):
    B, L, _ = qkv.shape
    nc = L // CHUNK
    ba_cb = (2 * DN_KEY + 2 * DN_VAL) // LANES
    tok = lambda w, cb: pl.BlockSpec((1, CHUNK, w), lambda b, j: (b, j, cb))
    mat = pl.BlockSpec((1, 1, DN_HEADS, DN_DK, DN_DV), lambda b, j: (b, j, 0, 0, 0))
    return pl.pallas_call(
        functools.partial(_dn_chunk_kernel, rev=rev),
        grid=(B, nc),
        in_specs=[tok(DN_KEY, 0), tok(DN_KEY, 1), tok(DN_VAL, 2), tok(LANES, ba_cb),
                  pl.BlockSpec((2, LANES), lambda b, j: (0, 0))],
        out_specs=[mat, mat, pl.BlockSpec((1, 1, DN_HEADS, DN_DV), lambda b, j: (b, j, 0, 0)),
                   tok(DN_KEY, 0), tok(DN_VAL, 0)],
        out_shape=[jax.ShapeDtypeStruct((B, nc, DN_HEADS, DN_DK, DN_DV), BF16),
                   jax.ShapeDtypeStruct((B, nc, DN_HEADS, DN_DK, DN_DV), F32),
                   jax.ShapeDtypeStruct((B, nc, DN_HEADS, DN_DV), F32),
                   jax.ShapeDtypeStruct((B, L, DN_KEY), BF16),
                   jax.ShapeDtypeStruct((B, L, DN_VAL), F32)],
        compiler_params=_cparams(("parallel", "parallel")),
        name="dn_chunk_rev" if rev else "dn_chunk_fwd",
    )(qkv, qkv, qkv, proj, par)


def _dn_seq_kernel(*refs, nc, with_prev):
    if with_prev:
        p_ref, qm_ref, g_ref, r_ref, oi_ref, s0_ref, prev_ref, o_ref, sfin_ref, s_scr = refs
    else:
        p_ref, qm_ref, g_ref, r_ref, oi_ref, s0_ref, o_ref, sfin_ref, s_scr = refs
        prev_ref = None
    j = pl.program_id(1)

    @pl.when(j == 0)
    def _():
        s_scr[...] = s0_ref[0]

    states = [s_scr[h] for h in range(DN_HEADS)]
    xs = []
    for h in range(DN_HEADS):
        lhs = jnp.concatenate([p_ref[0, 0, h], r_ref[0, :, h * DN_DK:(h + 1) * DN_DK]], axis=0)
        xs.append(_bdot(lhs, states[h]))
    for h in range(DN_HEADS):
        hs = slice(h * DN_DV, (h + 1) * DN_DV)
        s_scr[h] = g_ref[0, 0, h:h + 1, :] * states[h] + xs[h][:DN_DK] + qm_ref[0, 0, h]
        o_h = xs[h][DN_DK:] + oi_ref[0, :, hs]
        if prev_ref is not None:
            o_h = o_h + prev_ref[0, :, hs]
        o_ref[0, :, hs] = o_h

    @pl.when(j == nc - 1)
    def _():
        sfin_ref[0] = s_scr[...]


def _dn_seq(chunk_terms, s0, prev, rev):
    p, qm, g, r, oi = chunk_terms
    B, L, _ = r.shape
    nc = L // CHUNK
    with_prev = prev is not None

    def cidx(j):
        return nc - 1 - j if rev else j

    mat = pl.BlockSpec((1, 1, DN_HEADS, DN_DK, DN_DV), lambda b, j: (b, cidx(j), 0, 0, 0))
    tok = pl.BlockSpec((1, CHUNK, DN_VAL), lambda b, j: (b, cidx(j), 0))
    state = pl.BlockSpec((1, DN_HEADS, DN_DK, DN_DV), lambda b, j: (b, 0, 0, 0))
    specs = [mat, mat, pl.BlockSpec((1, 1, DN_HEADS, DN_DV), lambda b, j: (b, cidx(j), 0, 0)), tok, tok, state]
    args = [p, qm, g, r, oi, s0]
    if with_prev:
        specs.append(tok)
        args.append(prev)
    return pl.pallas_call(
        functools.partial(_dn_seq_kernel, nc=nc, with_prev=with_prev),
        grid=(B, nc),
        in_specs=specs,
        out_specs=[tok, state],
        out_shape=[jax.ShapeDtypeStruct((B, L, DN_VAL), F32),
                   jax.ShapeDtypeStruct((B, DN_HEADS, DN_DK, DN_DV), F32)],
        scratch_shapes=[pltpu.VMEM((DN_HEADS, DN_DK, DN_DV), F32)],
        compiler_params=_cparams(("parallel", "arbitrary")),
        name="dn_seq_rev" if rev else "dn_seq_fwd",
    )(*args)


def _delta_mixer(u_c, u_x, w_in, conv_w, A_log, dt_bias, norm_g, w_out, ctx_out):
    z_cols = slice(2 * DN_KEY + DN_VAL, 2 * DN_KEY + 2 * DN_VAL)
    a_lanes = slice(2 * DN_HEADS, 4 * DN_HEADS)
    par = jnp.zeros((2, LANES), F32)
    par = par.at[0, a_lanes].set(-jnp.exp(A_log).reshape(-1)).at[1, a_lanes].set(dt_bias.reshape(-1))

    def bidir(proj, s0f, s0b):
        qkv = _dn_prep(proj, conv_w)
        o, s_f = _dn_seq(_dn_chunk(qkv, proj, par, False), s0f, None, False)
        o, s_b = _dn_seq(_dn_chunk(qkv, proj, par, True), s0b, o, True)
        return o, s_f, s_b

    def readout(o, proj):
        y = _head_rms_norm(o, norm_g, DN_HEADS) * jax.nn.silu(proj[..., z_cols])
        return _mm3(y, w_out)

    pc = _mm3(u_c, w_in)
    s0 = jnp.zeros((u_c.shape[0], DN_HEADS, DN_DK, DN_DV), F32)
    o_c, s_f, s_b = bidir(pc, s0, s0)
    px = _mm3(u_x, w_in)
    o_x, _, _ = bidir(px, s_f, s_b)
    y_c = readout(o_c, pc) if ctx_out else None
    return y_c, readout(o_x, px)


def _hyena_filter_spectrum(L, pe_w1, pe_b1, freq, pe_w2, pe_b2, pe_w3, decay):
    t = jnp.arange(L, dtype=F32)
    tn = t / L
    ang = (2.0 * math.pi / L) * t[:, None] * jnp.arange(1, HY_BANDS + 1, dtype=F32)[None, :]
    feats = jnp.concatenate([tn[:, None], jnp.cos(ang), jnp.sin(ang)], -1)
    h = jnp.sin(freq * (feats @ pe_w1 + pe_b1))
    h = jnp.sin(freq * (h @ pe_w2 + pe_b2))
    h = (h @ pe_w3) * (jnp.exp(-tn[:, None] * decay) + HY_SHIFT)
    h = h.reshape(L, HY_ORDER, HY_SIDES, D_MODEL)
    h_circ = jnp.concatenate([h[:, :, 0], jnp.zeros((1, HY_ORDER, D_MODEL), F32), h[:0:-1, :, 1]], axis=0)
    return jnp.fft.rfft(h_circ, axis=0)


def _fft_conv(u, spec):
    L = u.shape[1]
    return jnp.fft.irfft(jnp.fft.rfft(u, n=2 * L, axis=1) * spec[None], n=2 * L, axis=1)[:, :L]


def _hyena_seq(u, w_in, conv_w, filt, skip, w_out):
    L = u.shape[1]
    x1, x2, v = jnp.split(_dwconv(_mm3(u, w_in), conv_w), 3, axis=-1)
    spec = _hyena_filter_spectrum(L, *filt)
    z = x1 * (_fft_conv(v, spec[:, 0]) + skip[0] * v)
    y = x2 * (_fft_conv(z, spec[:, 1]) + skip[1] * z)
    return _mm3(y, w_out)


def _swiglu(h, w_gu, w_down):
    B, L, D = h.shape
    M = B * L
    tm = _pick(M, (512, 256, 128))
    te = jnp.zeros((M // tm,), jnp.int32)
    return _ffn(h.reshape(M, D), w_gu.astype(BF16)[None], w_down.astype(BF16)[None], te, tm).reshape(B, L, D)


def _moe(h, router, w_gu, w_down):
    B, L, D = h.shape
    M = B * L
    tm = MOE_TILE
    hf = h.reshape(M, D)
    logits = jnp.dot(hf, router, precision=HIGHEST)
    top_val, top_idx = lax.top_k(logits, TOP_K)
    weights = jax.nn.softmax(top_val, axis=-1)
    e_flat = top_idx.reshape(-1).astype(jnp.int32)
    onehot = (e_flat[:, None] == jnp.arange(N_EXPERTS, dtype=jnp.int32)[None, :]).astype(jnp.int32)
    rank = jnp.sum((jnp.cumsum(onehot, axis=0) - onehot) * onehot, axis=1)
    counts = jnp.sum(onehot, axis=0)
    padded = ((counts + tm - 1) // tm) * tm
    ends = jnp.cumsum(padded)
    pos = (ends - padded)[e_flat] + rank
    P = M * TOP_K + N_EXPERTS * tm
    src = jnp.zeros((P,), jnp.int32).at[pos].set(jnp.arange(M * TOP_K, dtype=jnp.int32) // TOP_K)
    tile_expert = jnp.minimum(jnp.searchsorted(ends, jnp.arange(P // tm, dtype=jnp.int32) * tm, side='right'),
                              N_EXPERTS - 1).astype(jnp.int32)
    xs = hf.astype(BF16)[src]
    ys = _ffn(xs, w_gu.astype(BF16), w_down.astype(BF16), tile_expert, tm)
    y = ys[pos].reshape(M, TOP_K, D)
    return jnp.sum(y * weights[..., None], axis=1).reshape(B, L, D)


def kernel(x, c, ctx, c_ctx, ada_w, ada_b, ln_g, ln_b,
           gla_w_in, gla_gate_w1, gla_gate_w2, gla_gate_b, gla_norm_g, gla_w_out,
           dn_w_in, dn_conv, dn_A_log, dn_dt_bias, dn_norm_g, dn_w_out,
           hy_w_in, hy_conv, hy_pe_w1, hy_pe_b1, hy_freq, hy_pe_w2, hy_pe_b2, hy_pe_w3,
           hy_decay, hy_skip, hy_w_out,
           ffn_w_gu, ffn_w_down, moe_router, moe_w_gu, moe_w_down):
    ROWS = x.shape[1] // GRID_W
    x = x + _grid_sincos(ROWS, x.dtype)[None]
    xc = ctx
    cond_x = jax.nn.silu(c)
    cond_c = jax.nn.silu(c_ctx)
    for i in range(DEPTH):
        last = i == DEPTH - 1
        kind = i % N_MIXERS
        j = i // N_MIXERS
        mod_x = jnp.dot(cond_x, ada_w[i], precision=HIGHEST) + ada_b[i]
        mod_c = jnp.dot(cond_c, ada_w[i], precision=HIGHEST) + ada_b[i]
        sh1, sc1, g1, sh2, sc2, g2 = jnp.split(mod_x[:, None, :], 6, axis=-1)
        csh1, csc1, cg1, csh2, csc2, cg2 = jnp.split(mod_c, 6, axis=-1)
        u_x = x * (1 + sc1) + sh1
        u_c = None if (last and kind == 2) else xc * (1 + csc1) + csh1
        if kind == 0:
            y_c, y_x = _gla_mixer(u_c, u_x, gla_w_in[j], gla_gate_w1[j], gla_gate_w2[j], gla_gate_b[j],
                                  gla_norm_g[j], gla_w_out[j], not last)
        elif kind == 1:
            y_c, y_x = _delta_mixer(u_c, u_x, dn_w_in[j], dn_conv[j], dn_A_log[j], dn_dt_bias[j],
                                    dn_norm_g[j], dn_w_out[j], not last)
        else:
            filt = (hy_pe_w1[j], hy_pe_b1[j], hy_freq[j], hy_pe_w2[j], hy_pe_b2[j], hy_pe_w3[j], hy_decay[j])
            y_x = _hyena_seq(u_x, hy_w_in[j], hy_conv[j], filt, hy_skip[j], hy_w_out[j])
            y_c = None if last else _hyena_seq(u_c, hy_w_in[j], hy_conv[j], filt, hy_skip[j], hy_w_out[j])
        x = _layer_norm(DEEPNORM_ALPHA * x + g1 * y_x, ln_g[i, 0], ln_b[i, 0])
        if not last:
            xc = _layer_norm(DEEPNORM_ALPHA * xc + cg1 * y_c, ln_g[i, 0], ln_b[i, 0])

        def channel_mix(h):
            if i % 2 == 0:
                return _swiglu(h, ffn_w_gu[i // 2], ffn_w_down[i // 2])
            return _moe(h, moe_router[i // 2], moe_w_gu[i // 2], moe_w_down[i // 2])

        x = _layer_norm(DEEPNORM_ALPHA * x + g2 * channel_mix(x * (1 + sc2) + sh2), ln_g[i, 1], ln_b[i, 1])
        if not last:
            xc = _layer_norm(DEEPNORM_ALPHA * xc + cg2 * channel_mix(xc * (1 + csc2) + csh2), ln_g[i, 1], ln_b[i, 1])
    return x
```

```python
import functools
import math

import jax
import jax.numpy as jnp
from jax import lax
from jax.experimental import pallas as pl
from jax.experimental.pallas import tpu as pltpu

F32 = jnp.float32
BF16 = jnp.bfloat16
HIGHEST = lax.Precision.HIGHEST

D_MODEL = 1024
DEPTH = 4
GRID_W = 64
LN_EPS = 1e-5
NORM_EPS = 1e-6
DEEPNORM_ALPHA = (2.0 * DEPTH) ** 0.25
CHUNK = 64
N_MIXERS = 3
LANES = 128

GLA_HEADS = 4
GLA_DK = D_MODEL // (2 * GLA_HEADS)
GLA_DV = D_MODEL // GLA_HEADS
GLA_KEY = GLA_HEADS * GLA_DK
GLA_VAL = GLA_HEADS * GLA_DV
GLA_RANK = 16
GLA_TAU = 16.0

DN_HEADS = 8
DN_DK = D_MODEL // DN_HEADS
DN_DV = D_MODEL // DN_HEADS
DN_KEY = DN_HEADS * DN_DK
DN_VAL = DN_HEADS * DN_DV

HY_ORDER = 2
HY_SIDES = 2
HY_BANDS = 16
HY_SHIFT = 0.05

N_EXPERTS = 8
TOP_K = 2
MOE_TILE = 512

VMEM_LIMIT_BYTES = 56 * 1024 * 1024

_NT = (((1,), (1,)), ((), ()))
_TN = (((0,), (0,)), ((), ()))


def _cparams(sem):
    return pltpu.CompilerParams(dimension_semantics=sem, vmem_limit_bytes=VMEM_LIMIT_BYTES)


def _mm_kernel(a_ref, b_ref, o_ref, *, nk):
    k = pl.program_id(2)
    part = jnp.dot(a_ref[...].astype(BF16), b_ref[...], preferred_element_type=F32)
    if nk == 1:
        o_ref[...] = part
    else:
        @pl.when(k == 0)
        def _():
            o_ref[...] = part

        @pl.when(k > 0)
        def _():
            o_ref[...] += part


def _pick(n, cands):
    for c in cands:
        if n % c == 0:
            return c
    return n


def _lane_tile(n, cap):
    best = LANES
    for t in range(LANES, cap + 1, LANES):
        if n % t == 0:
            best = t
    return best


def _mm(a, w):
    M, K = a.shape
    N = w.shape[1]
    n_pad = (-N) % LANES
    wb = w.astype(BF16)
    if n_pad:
        wb = jnp.pad(wb, ((0, 0), (0, n_pad)))
    Np = N + n_pad
    tm = _pick(M, (1024, 512, 256, 128, 64, 32, 16, 8))
    tn = _lane_tile(Np, 1408)
    tk = _pick(K, (1024, 512, 256, 128))
    nk = K // tk
    return pl.pallas_call(
        functools.partial(_mm_kernel, nk=nk),
        grid=(M // tm, Np // tn, nk),
        in_specs=[pl.BlockSpec((tm, tk), lambda i, j, k: (i, k)),
                  pl.BlockSpec((tk, tn), lambda i, j, k: (k, j))],
        out_specs=pl.BlockSpec((tm, tn), lambda i, j, k: (i, j)),
        out_shape=jax.ShapeDtypeStruct((M, Np), F32),
        compiler_params=_cparams(("parallel", "parallel", "arbitrary")),
        name="mm",
    )(a, wb)


def _mm3(u, w):
    B, L, K = u.shape
    return _mm(u.reshape(B * L, K), w).reshape(B, L, -1)


def _ffn_kernel(te_ref, u_ref, wg_ref, wu_ref, wd_ref, o_ref, acc_ref, *, nf):
    f = pl.program_id(1)

    @pl.when(f == 0)
    def _():
        acc_ref[...] = jnp.zeros_like(acc_ref)

    u = u_ref[...].astype(BF16)
    g = jnp.dot(u, wg_ref[0], preferred_element_type=F32)
    up = jnp.dot(u, wu_ref[0], preferred_element_type=F32)
    act = (g * jax.nn.sigmoid(g) * up).astype(BF16)
    acc_ref[...] += jnp.dot(act, wd_ref[0], preferred_element_type=F32)

    @pl.when(f == nf - 1)
    def _():
        o_ref[...] = acc_ref[...]


def _ffn(u, w_gu, w_down, tile_expert, tm):
    M, D = u.shape
    F = w_down.shape[1]
    tf = _pick(F, (512, 256, 128))
    nf = F // tf
    grid_spec = pltpu.PrefetchScalarGridSpec(
        num_scalar_prefetch=1,
        grid=(M // tm, nf),
        in_specs=[pl.BlockSpec((tm, D), lambda i, f, te: (i, 0)),
                  pl.BlockSpec((1, D, tf), lambda i, f, te: (te[i], 0, f)),
                  pl.BlockSpec((1, D, tf), lambda i, f, te: (te[i], 0, f + nf)),
                  pl.BlockSpec((1, tf, D), lambda i, f, te: (te[i], f, 0))],
        out_specs=pl.BlockSpec((tm, D), lambda i, f, te: (i, 0)),
        scratch_shapes=[pltpu.VMEM((tm, D), F32)])
    return pl.pallas_call(
        functools.partial(_ffn_kernel, nf=nf),
        grid_spec=grid_spec,
        out_shape=jax.ShapeDtypeStruct((M, D), F32),
        compiler_params=_cparams(("parallel", "arbitrary")),
        name="ffn",
    )(tile_expert, u, w_gu, w_gu, w_down)


def _layer_norm(x, g, b):
    xc = x - jnp.mean(x, -1, keepdims=True)
    var = jnp.mean(xc * xc, -1, keepdims=True)
    return xc * lax.rsqrt(var + LN_EPS) * g + b


def _head_rms_norm(o, g, heads):
    B, L, W = o.shape
    oh = o.reshape(B, L, heads, W // heads)
    oh = oh * lax.rsqrt(jnp.mean(oh * oh, -1, keepdims=True) + NORM_EPS) * g
    return oh.reshape(B, L, W)


def _grid_sincos(rows, dtype):
    r, col = jnp.meshgrid(jnp.arange(rows, dtype=F32), jnp.arange(GRID_W, dtype=F32), indexing='ij')
    n_freq = D_MODEL // 4
    omega = 1.0 / (10000.0 ** (jnp.arange(n_freq, dtype=F32) / n_freq))

    def emb(p):
        ang = p.reshape(-1)[:, None] * omega[None, :]
        return jnp.concatenate([jnp.sin(ang), jnp.cos(ang)], -1)

    return jnp.concatenate([emb(r), emb(col)], -1).astype(dtype)


def _chunk_masks(rev):
    row = lax.broadcasted_iota(jnp.int32, (CHUNK, CHUNK), 0)
    col = lax.broadcasted_iota(jnp.int32, (CHUNK, CHUNK), 1)
    if rev:
        return row, col, col >= row, col > row
    return row, col, col <= row, col < row


def _bdot(a, b, dims=None):
    a = a.astype(BF16)
    b = b.astype(BF16)
    if dims is None:
        return jnp.dot(a, b, preferred_element_type=F32)
    return lax.dot_general(a, b, dims, preferred_element_type=F32)


def _scan_specs(nc, rev, widths_and_blocks):
    def cidx(j):
        return nc - 1 - j if rev else j
    return [pl.BlockSpec((1, CHUNK, w), functools.partial(lambda b, j, cb: (b, cidx(j), cb), cb=cb))
            for w, cb in widths_and_blocks]


def _gla_kernel(*refs, rev, nc, with_prev):
    if with_prev:
        q_ref, k_ref, v_ref, t_ref, w2_ref, gb_ref, s0_ref, prev_ref, o_ref, sfin_ref, s_scr = refs
    else:
        q_ref, k_ref, v_ref, t_ref, w2_ref, gb_ref, s0_ref, o_ref, sfin_ref, s_scr = refs
        prev_ref = None
    j = pl.program_id(1)

    @pl.when(j == 0)
    def _():
        s_scr[...] = s0_ref[0]

    _, _, incl, _ = _chunk_masks(rev)
    z = _bdot(t_ref[0], w2_ref[...]) + gb_ref[...]
    log_a = (jnp.minimum(z, 0.0) - jnp.log(1.0 + jnp.exp(-jnp.abs(z)))) * (1.0 / GLA_TAU)
    b = jnp.dot(incl.astype(F32), log_a, precision=HIGHEST, preferred_element_type=F32)
    last = 0 if rev else CHUNK - 1
    heads = range(GLA_HEADS)
    ks = [slice(h * GLA_DK, (h + 1) * GLA_DK) for h in heads]
    vs = [slice(h * GLA_DV, (h + 1) * GLA_DV) for h in heads]
    bhs = [b[:, ks[h]] for h in heads]
    b_lasts = [bh[last:last + 1, :] for bh in bhs]
    kfs = [k_ref[0, :, ks[h]] for h in heads]
    vbs = [v_ref[0, :, vs[h]].astype(BF16) for h in heads]
    sts = [s_scr[h] for h in heads]
    q_decs = [(q_ref[0, :, ks[h]] * (GLA_DK ** -0.5) * jnp.exp(bhs[h])).astype(BF16) for h in heads]
    k_invs = [(kfs[h] * jnp.exp(-bhs[h])).astype(BF16) for h in heads]
    scores = [jnp.where(incl, _bdot(q_decs[h], k_invs[h], _NT), 0.0) for h in heads]
    inter = [_bdot(q_decs[h], sts[h], _NT) for h in heads]
    intra = [_bdot(scores[h], vbs[h]) for h in heads]
    kv = [_bdot(vbs[h], kfs[h] * jnp.exp(b_lasts[h] - bhs[h]), _TN) for h in heads]
    for h in heads:
        s_scr[h] = sts[h] * jnp.exp(b_lasts[h]) + kv[h]
        o_h = intra[h] + inter[h]
        if prev_ref is not None:
            o_h = o_h + prev_ref[0, :, vs[h]]
        o_ref[0, :, vs[h]] = o_h

    @pl.when(j == nc - 1)
    def _():
        sfin_ref[0] = s_scr[...]


def _gla_scan(proj, w2, gb, s0, prev, rev):
    B, L, _ = proj.shape
    nc = L // CHUNK
    with_prev = prev is not None
    gate_cb = (2 * GLA_KEY + 2 * GLA_VAL) // LANES
    specs = _scan_specs(nc, rev, [(GLA_KEY, 0), (GLA_KEY, 1), (GLA_VAL, 1), (LANES, gate_cb)])
    specs += [pl.BlockSpec((LANES, GLA_KEY), lambda b, j: (0, 0)),
              pl.BlockSpec((1, GLA_KEY), lambda b, j: (0, 0)),
              pl.BlockSpec((1, GLA_HEADS, GLA_DV, GLA_DK), lambda b, j: (b, 0, 0, 0))]
    args = [proj, proj, proj, proj, w2, gb, s0]
    o_spec = _scan_specs(nc, rev, [(GLA_VAL, 0)])[0]
    if with_prev:
        specs.append(o_spec)
        args.append(prev)
    return pl.pallas_call(
        functools.partial(_gla_kernel, rev=rev, nc=nc, with_prev=with_prev),
        grid=(B, nc),
        in_specs=specs,
        out_specs=[o_spec, pl.BlockSpec((1, GLA_HEADS, GLA_DV, GLA_DK), lambda b, j: (b, 0, 0, 0))],
        out_shape=[jax.ShapeDtypeStruct((B, L, GLA_VAL), F32),
                   jax.ShapeDtypeStruct((B, GLA_HEADS, GLA_DV, GLA_DK), F32)],
        scratch_shapes=[pltpu.VMEM((GLA_HEADS, GLA_DV, GLA_DK), F32)],
        compiler_params=_cparams(("parallel", "arbitrary")),
        name="gla_scan_rev" if rev else "gla_scan_fwd",
    )(*args)


def _gla_mixer(u_c, u_x, w_in, gate_w1, gate_w2, gate_b, norm_g, w_out, ctx_out):
    w_ext = jnp.concatenate([w_in, gate_w1[0], gate_w1[1]], axis=1)
    r_cols = slice(2 * GLA_KEY + GLA_VAL, 2 * GLA_KEY + 2 * GLA_VAL)
    w2 = []
    for d in range(2):
        w2.append(jnp.zeros((LANES, GLA_KEY), F32).at[d * GLA_RANK:(d + 1) * GLA_RANK].set(gate_w2[d]).astype(BF16))
    gb = gate_b[:, None, :]

    def bidir(proj, s0f, s0b):
        o, s_f = _gla_scan(proj, w2[0], gb[0], s0f, None, False)
        o, s_b = _gla_scan(proj, w2[1], gb[1], s0b, o, True)
        return o, s_f, s_b

    def readout(o, proj):
        y = _head_rms_norm(o, norm_g, GLA_HEADS) * jax.nn.silu(proj[..., r_cols])
        return _mm3(y, w_out)

    pc = _mm3(u_c, w_ext)
    s0 = jnp.zeros((u_c.shape[0], GLA_HEADS, GLA_DV, GLA_DK), F32)
    o_c, s_f, s_b = bidir(pc, s0, s0)
    px = _mm3(u_x, w_ext)
    o_x, _, _ = bidir(px, s_f, s_b)
    y_c = readout(o_c, pc) if ctx_out else None
    return y_c, readout(o_x, px)


def _dn_prep_kernel(x_ref, w_ref, o_ref, *, L):
    cb = pl.program_id(1)
    y = _dwconv3_rows(x_ref[0], w_ref[...])
    y = y * jax.nn.sigmoid(y)
    inv = lax.rsqrt(jnp.sum(y * y, axis=1, keepdims=True) + NORM_EPS)
    q_scale = jnp.where(cb < DN_HEADS, DN_DK ** -0.5, 1.0)
    fac = jnp.where(cb < 2 * DN_HEADS, inv * q_scale, 1.0)
    o_ref[0] = (y * fac).astype(BF16)


def _dn_prep(proj, conv_w):
    B, L, _ = proj.shape
    ncb = (2 * DN_KEY + DN_VAL) // LANES
    return pl.pallas_call(
        functools.partial(_dn_prep_kernel, L=L),
        grid=(B, ncb),
        in_specs=[pl.BlockSpec((1, L, LANES), lambda b, c: (b, 0, c)),
                  pl.BlockSpec((conv_w.shape[0], LANES), lambda b, c: (0, c))],
        out_specs=pl.BlockSpec((1, L, LANES), lambda b, c: (b, 0, c)),
        out_shape=jax.ShapeDtypeStruct((B, L, ncb * LANES), BF16),
        compiler_params=_cparams(("parallel", "parallel")),
        name="dn_prep",
    )(proj, conv_w)


DN_GROUP = 4
DN_ROWS = DN_GROUP * CHUNK


DN_CPS = 2
DN_LEVELS = CHUNK.bit_length() - 1
M_INCL, M_STRICT, M_EYE, M_LEVEL0 = 0, 1, 2, 3


def _dn_masks(rev):
    r = jnp.arange(DN_ROWS, dtype=jnp.int32)[:, None]
    c = jnp.arange(DN_ROWS, dtype=jnp.int32)[None, :]
    same_head = (r // CHUNK) == (c // CHUNK)
    rt, ct = r % CHUNK, c % CHUNK
    out = [same_head & ((ct >= rt) if rev else (ct <= rt)), same_head & ((ct > rt) if rev else (ct < rt)), r == c]
    for lv in range(DN_LEVELS):
        s = 1 << lv
        r_hi, c_hi = (r // s) % 2 == 1, (c // s) % 2 == 1
        out.append(((r // (2 * s)) == (c // (2 * s))) & ((c_hi & ~r_hi) if rev else (r_hi & ~c_hi)))
    return jnp.stack(out).astype(F32)


def _dn_chunk_kernel(q_ref, k_ref, v_ref, ba_ref, par_ref, m_ref, p_ref, qm_ref, g_ref, r_ref, oi_ref, *, rev):
    prow = lax.broadcasted_iota(jnp.int32, (2 * CHUNK, CHUNK), 0)
    pcol = lax.broadcasted_iota(jnp.int32, (2 * CHUNK, CHUNK), 1)
    tri_pad = (((pcol >= prow) if rev else (pcol <= prow)) & (prow < CHUNK)).astype(F32)
    d = 1 if rev else 0
    last = 0 if rev else CHUNK - 1
    incl = m_ref[M_INCL]
    eye = m_ref[M_EYE]
    items = []
    for ci in range(DN_CPS):
        tr = slice(ci * CHUNK, (ci + 1) * CHUNK)
        ba = ba_ref[0, tr, :]
        beta_all = jax.nn.sigmoid(ba)
        sp_in = ba + par_ref[1:2, :]
        softplus = jnp.maximum(sp_in, 0.0) + jnp.log(1.0 + jnp.exp(-jnp.abs(sp_in)))
        g_all = par_ref[0:1, :] * softplus
        gc_pad = jnp.dot(tri_pad, g_all, precision=HIGHEST, preferred_element_type=F32)
        gc = gc_pad[:CHUNK]
        gc_t = gc_pad.T
        for grp in range(DN_HEADS // DN_GROUP):
            heads = range(grp * DN_GROUP, (grp + 1) * DN_GROUP)
            gls = [2 * DN_HEADS + d * DN_HEADS + h for h in heads]
            qs = jnp.concatenate([q_ref[0, tr, h * DN_DK:(h + 1) * DN_DK] for h in heads], axis=0)
            ks = jnp.concatenate([k_ref[0, tr, h * DN_DK:(h + 1) * DN_DK] for h in heads], axis=0)
            vs = jnp.concatenate([v_ref[0, tr, h * DN_DV:(h + 1) * DN_DV] for h in heads], axis=0)
            beta = jnp.concatenate([beta_all[:, d * DN_HEADS + h:d * DN_HEADS + h + 1] for h in heads], axis=0)
            gcol = jnp.concatenate([gc[:, gl:gl + 1] for gl in gls], axis=0)
            glast = jnp.concatenate([jnp.broadcast_to(gc[last:last + 1, gl:gl + 1], (CHUNK, 1)) for gl in gls],
                                    axis=0)
            rows = [gc_t[gl:gl + 1, :] for gl in gls]
            grow = jnp.concatenate([rows[0] + pltpu.roll(rows[1], CHUNK, 1),
                                    rows[2] + pltpu.roll(rows[3], CHUNK, 1)], axis=1)
            items.append(dict(ci=ci, tr=tr, heads=heads, qs=qs, ks=ks, vs=vs, beta=beta, gcol=gcol, glast=glast,
                              grow=grow, gfin=[gc[last:last + 1, gl:gl + 1] for gl in gls]))
    dms = [jnp.exp(jnp.minimum(it["gcol"] - it["grow"], 0.0)) * incl for it in items]
    kfs = [it["ks"].astype(F32) for it in items]
    kbetas = [kf * it["beta"] for kf, it in zip(kfs, items)]
    a_s = [_bdot(kb, it["ks"], _NT) * (dm * m_ref[M_STRICT]) for kb, it, dm in zip(kbetas, items, dms)]
    ts = [eye - a * m_ref[M_LEVEL0] for a in a_s]
    for lv in range(1, DN_LEVELS):
        mts = [_bdot(a * m_ref[M_LEVEL0 + lv], t) for a, t in zip(a_s, ts)]
        ts = [t - _bdot(t, mt) for t, mt in zip(ts, mts)]
    rhs = [jnp.concatenate([it["vs"].astype(F32) * it["beta"], kb * jnp.exp(it["gcol"])], axis=1)
           for it, kb in zip(items, kbetas)]
    sols = [_bdot(t, r) for t, r in zip(ts, rhs)]
    qks = [_bdot(it["qs"], it["ks"], _NT) * dm for it, dm in zip(items, dms)]
    qsols = [_bdot(qk, sol) for qk, sol in zip(qks, sols)]
    for it, kf, sol, qsol in zip(items, kfs, sols, qsols):
        ci, tr = it["ci"], it["tr"]
        k_dec = kf * jnp.exp(it["glast"] - it["gcol"])
        r_all = it["qs"].astype(F32) * jnp.exp(it["gcol"]) - qsol[:, DN_DV:]
        for i, h in enumerate(it["heads"]):
            rs = slice(i * CHUNK, (i + 1) * CHUNK)
            hs = slice(h * DN_DV, (h + 1) * DN_DV)
            pq = _bdot(k_dec[rs], sol[rs], _TN)
            qm_ref[0, ci, h] = pq[:, :DN_DV]
            p_ref[0, ci, h] = (-pq[:, DN_DV:]).astype(BF16)
            g_ref[0, ci, h:h + 1, :] = jnp.broadcast_to(jnp.exp(it["gfin"][i]), (1, DN_DV))
            r_ref[0, tr, hs] = r_all[rs].astype(BF16)
            oi_ref[0, tr, hs] = qsol[rs, :DN_DV]


def _dn_chunk(qkv, proj, par, rev):
    B, L, _ = qkv.shape
    nc = L // CHUNK
    assert nc % DN_CPS == 0
    ba_cb = (2 * DN_KEY + 2 * DN_VAL) // LANES
    tok = lambda w, cb: pl.BlockSpec((1, DN_CPS * CHUNK, w), lambda b, j: (b, j, cb))
    mat = pl.BlockSpec((1, DN_CPS, DN_HEADS, DN_DK, DN_DV), lambda b, j: (b, j, 0, 0, 0))
    masks = _dn_masks(rev)
    return pl.pallas_call(
        functools.partial(_dn_chunk_kernel, rev=rev),
        grid=(B, nc // DN_CPS),
        in_specs=[tok(DN_KEY, 0), tok(DN_KEY, 1), tok(DN_VAL, 2), tok(LANES, ba_cb),
                  pl.BlockSpec((2, LANES), lambda b, j: (0, 0)),
                  pl.BlockSpec(masks.shape, lambda b, j: (0, 0, 0), pipeline_mode=pl.Buffered(1))],
        out_specs=[mat, mat, pl.BlockSpec((1, DN_CPS, DN_HEADS, DN_DV), lambda b, j: (b, j, 0, 0)),
                   tok(DN_KEY, 0), tok(DN_VAL, 0)],
        out_shape=[jax.ShapeDtypeStruct((B, nc, DN_HEADS, DN_DK, DN_DV), BF16),
                   jax.ShapeDtypeStruct((B, nc, DN_HEADS, DN_DK, DN_DV), F32),
                   jax.ShapeDtypeStruct((B, nc, DN_HEADS, DN_DV), F32),
                   jax.ShapeDtypeStruct((B, L, DN_KEY), BF16),
                   jax.ShapeDtypeStruct((B, L, DN_VAL), F32)],
        compiler_params=_cparams(("parallel", "parallel")),
        name="dn_chunk_rev" if rev else "dn_chunk_fwd",
    )(qkv, qkv, qkv, proj, par, masks)


def _dn_seq_kernel(*refs, nc, with_prev):
    if with_prev:
        p_ref, qm_ref, g_ref, r_ref, oi_ref, s0_ref, prev_ref, o_ref, sfin_ref, s_scr = refs
    else:
        p_ref, qm_ref, g_ref, r_ref, oi_ref, s0_ref, o_ref, sfin_ref, s_scr = refs
        prev_ref = None
    j = pl.program_id(1)

    @pl.when(j == 0)
    def _():
        s_scr[...] = s0_ref[0]

    states = [s_scr[h] for h in range(DN_HEADS)]
    xs = []
    for h in range(DN_HEADS):
        lhs = jnp.concatenate([p_ref[0, 0, h], r_ref[0, :, h * DN_DK:(h + 1) * DN_DK]], axis=0)
        xs.append(_bdot(lhs, states[h]))
    for h in range(DN_HEADS):
        hs = slice(h * DN_DV, (h + 1) * DN_DV)
        s_scr[h] = g_ref[0, 0, h:h + 1, :] * states[h] + xs[h][:DN_DK] + qm_ref[0, 0, h]
        o_h = xs[h][DN_DK:] + oi_ref[0, :, hs]
        if prev_ref is not None:
            o_h = o_h + prev_ref[0, :, hs]
        o_ref[0, :, hs] = o_h

    @pl.when(j == nc - 1)
    def _():
        sfin_ref[0] = s_scr[...]


def _dn_seq(chunk_terms, s0, prev, rev):
    p, qm, g, r, oi = chunk_terms
    B, L, _ = r.shape
    nc = L // CHUNK
    with_prev = prev is not None

    def cidx(j):
        return nc - 1 - j if rev else j

    mat = pl.BlockSpec((1, 1, DN_HEADS, DN_DK, DN_DV), lambda b, j: (b, cidx(j), 0, 0, 0))
    tok = pl.BlockSpec((1, CHUNK, DN_VAL), lambda b, j: (b, cidx(j), 0))
    state = pl.BlockSpec((1, DN_HEADS, DN_DK, DN_DV), lambda b, j: (b, 0, 0, 0))
    specs = [mat, mat, pl.BlockSpec((1, 1, DN_HEADS, DN_DV), lambda b, j: (b, cidx(j), 0, 0)), tok, tok, state]
    args = [p, qm, g, r, oi, s0]
    if with_prev:
        specs.append(tok)
        args.append(prev)
    return pl.pallas_call(
        functools.partial(_dn_seq_kernel, nc=nc, with_prev=with_prev),
        grid=(B, nc),
        in_specs=specs,
        out_specs=[tok, state],
        out_shape=[jax.ShapeDtypeStruct((B, L, DN_VAL), F32),
                   jax.ShapeDtypeStruct((B, DN_HEADS, DN_DK, DN_DV), F32)],
        scratch_shapes=[pltpu.VMEM((DN_HEADS, DN_DK, DN_DV), F32)],
        compiler_params=_cparams(("parallel", "arbitrary")),
        name="dn_seq_rev" if rev else "dn_seq_fwd",
    )(*args)


def _delta_mixer(u_c, u_x, w_in, conv_w, A_log, dt_bias, norm_g, w_out, ctx_out):
    z_cols = slice(2 * DN_KEY + DN_VAL, 2 * DN_KEY + 2 * DN_VAL)
    a_lanes = slice(2 * DN_HEADS, 4 * DN_HEADS)
    par = jnp.zeros((2, LANES), F32)
    par = par.at[0, a_lanes].set(-jnp.exp(A_log).reshape(-1)).at[1, a_lanes].set(dt_bias.reshape(-1))

    def bidir(proj, s0f, s0b):
        qkv = _dn_prep(proj, conv_w)
        o, s_f = _dn_seq(_dn_chunk(qkv, proj, par, False), s0f, None, False)
        o, s_b = _dn_seq(_dn_chunk(qkv, proj, par, True), s0b, o, True)
        return o, s_f, s_b

    def readout(o, proj):
        y = _head_rms_norm(o, norm_g, DN_HEADS) * jax.nn.silu(proj[..., z_cols])
        return _mm3(y, w_out)

    pc = _mm3(u_c, w_in)
    s0 = jnp.zeros((u_c.shape[0], DN_HEADS, DN_DK, DN_DV), F32)
    o_c, s_f, s_b = bidir(pc, s0, s0)
    px = _mm3(u_x, w_in)
    o_x, _, _ = bidir(px, s_f, s_b)
    y_c = readout(o_c, pc) if ctx_out else None
    return y_c, readout(o_x, px)


def _hyena_filter_spectrum(L, pe_w1, pe_b1, freq, pe_w2, pe_b2, pe_w3, decay):
    t = jnp.arange(L, dtype=F32)
    tn = t / L
    ang = (2.0 * math.pi / L) * t[:, None] * jnp.arange(1, HY_BANDS + 1, dtype=F32)[None, :]
    feats = jnp.concatenate([tn[:, None], jnp.cos(ang), jnp.sin(ang)], -1)
    h = jnp.sin(freq * (feats @ pe_w1 + pe_b1))
    h = jnp.sin(freq * (h @ pe_w2 + pe_b2))
    h = (h @ pe_w3) * (jnp.exp(-tn[:, None] * decay) + HY_SHIFT)
    h = h.reshape(L, HY_ORDER, HY_SIDES, D_MODEL)
    h_circ = jnp.concatenate([h[:, :, 0], jnp.zeros((1, HY_ORDER, D_MODEL), F32), h[:0:-1, :, 1]], axis=0)
    return jnp.fft.rfft(h_circ, axis=0)


HY_TD = 256
HY_TK = 512


def _dwconv3_rows(x, w):
    L = x.shape[0]
    rowi = lax.broadcasted_iota(jnp.int32, x.shape, 0)
    x_prev = jnp.where(rowi == 0, 0.0, pltpu.roll(x, 1, 0))
    x_next = jnp.where(rowi == L - 1, 0.0, pltpu.roll(x, L - 1, 0))
    return w[0:1] * x_prev + w[1:2] * x + w[2:3] * x_next


def _hy_tables(L):
    n = 2 * L
    k = jnp.arange(L, dtype=jnp.int32)
    ang = (2.0 * math.pi / n) * ((k[:, None] * k[None, :]) % n).astype(F32)
    sign = 1.0 - 2.0 * (k % 2).astype(F32)
    ws = (-jnp.sin(ang)).at[0, :].set(sign)
    return jnp.cos(ang).astype(BF16), ws.astype(BF16), ws.T.astype(BF16)


def _hy_filter(L, filt):
    spec = _hyena_filter_spectrum(L, *filt)
    n = 2 * L
    scale = jnp.full((L, 1, 1), 2.0 / n, F32).at[0].set(1.0 / n)
    hr = jnp.real(spec[:L]) * scale
    hi = (jnp.imag(spec[:L]) * scale).at[0].set(0.0)
    hr2 = hr.at[0].set(jnp.real(spec[L]) / n)
    return tuple(jnp.transpose(a, (1, 0, 2)) for a in (hr, hi, hr2))


def _hy_fwd_kernel(x_ref, cw_ref, c_ref, ws_ref, hr_ref, hi_ref, hr2_ref, yr_ref, yi_ref, *, L, conv):
    x = x_ref[0]
    vb = (_dwconv3_rows(x, cw_ref[...]) if conv else x).astype(BF16)
    tk = min(HY_TK, L)
    for kb in range(L // tk):
        rows = slice(kb * tk, (kb + 1) * tk)
        ur = jnp.dot(c_ref[rows, :], vb, preferred_element_type=F32)
        ui = jnp.dot(ws_ref[rows, :], vb, preferred_element_type=F32)
        hr = hr_ref[rows, :]
        hi = hi_ref[rows, :]
        yr_ref[0, rows, :] = (ur * hr - ui * hi).astype(BF16)
        yi_ref[0, rows, :] = (ur * hi + ui * hr2_ref[rows, :]).astype(BF16)


def _hy_inv_kernel(yr_ref, yi_ref, c_ref, wst_ref, g_ref, gw_ref, v_ref, vw_ref, sk_ref, o_ref, *, L, conv_v):
    gate = _dwconv3_rows(g_ref[0], gw_ref[...])
    v = v_ref[0]
    if conv_v:
        v = _dwconv3_rows(v, vw_ref[...])
    yr = yr_ref[0]
    yi = yi_ref[0]
    tk = min(HY_TK, L)
    for tb in range(L // tk):
        rows = slice(tb * tk, (tb + 1) * tk)
        y = (jnp.dot(c_ref[rows, :], yr, preferred_element_type=F32)
             + jnp.dot(wst_ref[rows, :], yi, preferred_element_type=F32))
        o_ref[0, rows, :] = gate[rows, :] * (y + sk_ref[...] * v[rows, :])


def _hy_specs(L, td):
    table = pl.BlockSpec((L, L), lambda d, b: (0, 0), pipeline_mode=pl.Buffered(1))
    col = lambda off: pl.BlockSpec((1, L, td), functools.partial(lambda d, b, off: (b, 0, off + d), off=off))
    par = lambda r: pl.BlockSpec((r, td), lambda d, b: (0, d))
    return table, col, par


def _hy_fwd(x, col_off, conv_w, c, ws, hr, hi, hr2):
    B, L, _ = x.shape
    D = hr.shape[1]
    td = HY_TD
    table, col, par = _hy_specs(L, td)
    conv = conv_w is not None
    if not conv:
        conv_w = jnp.zeros((3, D), F32)
    out = jax.ShapeDtypeStruct((B, L, D), BF16)
    return pl.pallas_call(
        functools.partial(_hy_fwd_kernel, L=L, conv=conv),
        grid=(D // td, B),
        in_specs=[col(col_off), par(3), table, table, par(L), par(L), par(L)],
        out_specs=[col(0), col(0)],
        out_shape=[out, out],
        compiler_params=_cparams(("parallel", "parallel")),
        name="hy_fwd",
    )(x, conv_w, c, ws, hr, hi, hr2)


def _hy_inv(yr, yi, c, wst, proj, gate_off, gate_w, sig, sig_off, sig_w, skip):
    B, L, D = yr.shape
    td = HY_TD
    table, col, par = _hy_specs(L, td)
    conv_v = sig_w is not None
    if not conv_v:
        sig_w = jnp.zeros((3, D), F32)
    return pl.pallas_call(
        functools.partial(_hy_inv_kernel, L=L, conv_v=conv_v),
        grid=(D // td, B),
        in_specs=[col(0), col(0), table, table, col(gate_off), par(3), col(sig_off), par(3), par(1)],
        out_specs=col(0),
        out_shape=jax.ShapeDtypeStruct((B, L, D), F32),
        compiler_params=_cparams(("parallel", "parallel")),
        name="hy_inv",
    )(yr, yi, c, wst, proj, gate_w, sig, sig_w, skip)


def _hyena_seq(u, w_in, conv_w, filt, skip, w_out):
    B, L, D = u.shape
    assert conv_w.shape[0] == 3 and D % HY_TD == 0
    nd = D // HY_TD
    proj = _mm3(u, w_in)
    cw = [conv_w[:, i * D:(i + 1) * D] for i in range(3)]
    c, ws, wst = _hy_tables(L)
    hr, hi, hr2 = _hy_filter(L, filt)
    yr, yi = _hy_fwd(proj, 2 * nd, cw[2], c, ws, hr[0], hi[0], hr2[0])
    z = _hy_inv(yr, yi, c, wst, proj, 0, cw[0], proj, 2 * nd, cw[2], skip[0:1])
    yr, yi = _hy_fwd(z, 0, None, c, ws, hr[1], hi[1], hr2[1])
    y = _hy_inv(yr, yi, c, wst, proj, nd, cw[1], z, 0, None, skip[1:2])
    return _mm3(y, w_out)


def _swiglu(h, w_gu, w_down):
    B, L, D = h.shape
    M = B * L
    tm = _pick(M, (512, 256, 128))
    te = jnp.zeros((M // tm,), jnp.int32)
    return _ffn(h.reshape(M, D), w_gu.astype(BF16)[None], w_down.astype(BF16)[None], te, tm).reshape(B, L, D)


def _moe(h, router, w_gu, w_down):
    B, L, D = h.shape
    M = B * L
    tm = MOE_TILE
    hf = h.reshape(M, D)
    logits = jnp.dot(hf, router, precision=HIGHEST)
    top_val, top_idx = lax.top_k(logits, TOP_K)
    weights = jax.nn.softmax(top_val, axis=-1)
    e_flat = top_idx.reshape(-1).astype(jnp.int32)
    onehot = (e_flat[:, None] == jnp.arange(N_EXPERTS, dtype=jnp.int32)[None, :]).astype(jnp.int32)
    rank = jnp.sum((jnp.cumsum(onehot, axis=0) - onehot) * onehot, axis=1)
    counts = jnp.sum(onehot, axis=0)
    padded = ((counts + tm - 1) // tm) * tm
    ends = jnp.cumsum(padded)
    pos = (ends - padded)[e_flat] + rank
    P = M * TOP_K + N_EXPERTS * tm
    src = jnp.zeros((P,), jnp.int32).at[pos].set(jnp.arange(M * TOP_K, dtype=jnp.int32) // TOP_K)
    tile_expert = jnp.minimum(jnp.searchsorted(ends, jnp.arange(P // tm, dtype=jnp.int32) * tm, side='right'),
                              N_EXPERTS - 1).astype(jnp.int32)
    xs = hf.astype(BF16)[src]
    ys = _ffn(xs, w_gu.astype(BF16), w_down.astype(BF16), tile_expert, tm)
    y = ys[pos].reshape(M, TOP_K, D)
    return jnp.sum(y * weights[..., None], axis=1).reshape(B, L, D)


def kernel(x, c, ctx, c_ctx, ada_w, ada_b, ln_g, ln_b,
           gla_w_in, gla_gate_w1, gla_gate_w2, gla_gate_b, gla_norm_g, gla_w_out,
           dn_w_in, dn_conv, dn_A_log, dn_dt_bias, dn_norm_g, dn_w_out,
           hy_w_in, hy_conv, hy_pe_w1, hy_pe_b1, hy_freq, hy_pe_w2, hy_pe_b2, hy_pe_w3,
           hy_decay, hy_skip, hy_w_out,
           ffn_w_gu, ffn_w_down, moe_router, moe_w_gu, moe_w_down):
    ROWS = x.shape[1] // GRID_W
    x = x + _grid_sincos(ROWS, x.dtype)[None]
    xc = ctx
    cond_x = jax.nn.silu(c)
    cond_c = jax.nn.silu(c_ctx)
    for i in range(DEPTH):
        last = i == DEPTH - 1
        kind = i % N_MIXERS
        j = i // N_MIXERS
        mod_x = jnp.dot(cond_x, ada_w[i], precision=HIGHEST) + ada_b[i]
        mod_c = jnp.dot(cond_c, ada_w[i], precision=HIGHEST) + ada_b[i]
        sh1, sc1, g1, sh2, sc2, g2 = jnp.split(mod_x[:, None, :], 6, axis=-1)
        csh1, csc1, cg1, csh2, csc2, cg2 = jnp.split(mod_c, 6, axis=-1)
        u_x = x * (1 + sc1) + sh1
        u_c = None if (last and kind == 2) else xc * (1 + csc1) + csh1
        if kind == 0:
            y_c, y_x = _gla_mixer(u_c, u_x, gla_w_in[j], gla_gate_w1[j], gla_gate_w2[j], gla_gate_b[j],
                                  gla_norm_g[j], gla_w_out[j], not last)
        elif kind == 1:
            y_c, y_x = _delta_mixer(u_c, u_x, dn_w_in[j], dn_conv[j], dn_A_log[j], dn_dt_bias[j],
                                    dn_norm_g[j], dn_w_out[j], not last)
        else:
            filt = (hy_pe_w1[j], hy_pe_b1[j], hy_freq[j], hy_pe_w2[j], hy_pe_b2[j], hy_pe_w3[j], hy_decay[j])
            y_x = _hyena_seq(u_x, hy_w_in[j], hy_conv[j], filt, hy_skip[j], hy_w_out[j])
            y_c = None if last else _hyena_seq(u_c, hy_w_in[j], hy_conv[j], filt, hy_skip[j], hy_w_out[j])
        x = _layer_norm(DEEPNORM_ALPHA * x + g1 * y_x, ln_g[i, 0], ln_b[i, 0])
        if not last:
            xc = _layer_norm(DEEPNORM_ALPHA * xc + cg1 * y_c, ln_g[i, 0], ln_b[i, 0])

        def channel_mix(h):
            if i % 2 == 0:
                return _swiglu(h, ffn_w_gu[i // 2], ffn_w_down[i // 2])
            return _moe(h, moe_router[i // 2], moe_w_gu[i // 2], moe_w_down[i // 2])

        x = _layer_norm(DEEPNORM_ALPHA * x + g2 * channel_mix(x * (1 + sc2) + sh2), ln_g[i, 1], ln_b[i, 1])
        if not last:
            xc = _layer_norm(DEEPNORM_ALPHA * xc + cg2 * channel_mix(xc * (1 + csc2) + csh2), ln_g[i, 1], ln_b[i, 1])
    return x
```

```python
import functools
import math

import jax
import jax.numpy as jnp
from jax import lax
from jax.experimental import pallas as pl
from jax.experimental.pallas import tpu as pltpu

F32 = jnp.float32
BF16 = jnp.bfloat16
HIGHEST = lax.Precision.HIGHEST

D_MODEL = 1024
DEPTH = 4
GRID_W = 64
LN_EPS = 1e-5
NORM_EPS = 1e-6
DEEPNORM_ALPHA = (2.0 * DEPTH) ** 0.25
CHUNK = 64
N_MIXERS = 3
LANES = 128

GLA_HEADS = 4
GLA_DK = D_MODEL // (2 * GLA_HEADS)
GLA_DV = D_MODEL // GLA_HEADS
GLA_KEY = GLA_HEADS * GLA_DK
GLA_VAL = GLA_HEADS * GLA_DV
GLA_RANK = 16
GLA_TAU = 16.0

DN_HEADS = 8
DN_DK = D_MODEL // DN_HEADS
DN_DV = D_MODEL // DN_HEADS
DN_KEY = DN_HEADS * DN_DK
DN_VAL = DN_HEADS * DN_DV

HY_ORDER = 2
HY_SIDES = 2
HY_BANDS = 16
HY_SHIFT = 0.05

N_EXPERTS = 8
TOP_K = 2
MOE_TILE = 512

VMEM_LIMIT_BYTES = 56 * 1024 * 1024

_NT = (((1,), (1,)), ((), ()))
_TN = (((0,), (0,)), ((), ()))


def _cparams(sem):
    return pltpu.CompilerParams(dimension_semantics=sem, vmem_limit_bytes=VMEM_LIMIT_BYTES)


def _ln_rows(y, g, b):
    mu = jnp.mean(y, axis=-1, keepdims=True)
    yc = y - mu
    var = jnp.mean(yc * yc, axis=-1, keepdims=True)
    return yc * lax.rsqrt(var + LN_EPS) * g + b


def _mm_kernel(*refs, nk, mod, ln):
    refs = list(refs)
    a_ref, b_ref = refs[:2]
    o_ref = refs[-1]
    extra = refs[2:-1]
    a = a_ref[...]
    if mod:
        a = a * (1.0 + extra[0][0]) + extra[1][0]
        extra = extra[2:]
    part = jnp.dot(a.astype(BF16), b_ref[...], preferred_element_type=F32)
    if ln:
        res_ref, gate_ref, g_ref, bias_ref = extra
        o_ref[...] = _ln_rows(DEEPNORM_ALPHA * res_ref[...] + gate_ref[0] * part, g_ref[...], bias_ref[...])
    elif nk == 1:
        o_ref[...] = part
    else:
        k = pl.program_id(2)

        @pl.when(k == 0)
        def _():
            o_ref[...] = part

        @pl.when(k > 0)
        def _():
            o_ref[...] += part


def _pick(n, cands):
    for c in cands:
        if n % c == 0:
            return c
    return n


def _lane_tile(n, cap):
    best = LANES
    for t in range(LANES, cap + 1, LANES):
        if n % t == 0:
            best = t
    return best


def _mm(a, w, mod=None, ln=None):
    M, K = a.shape
    N = w.shape[1]
    n_pad = (-N) % LANES
    wb = w.astype(BF16)
    if n_pad:
        wb = jnp.pad(wb, ((0, 0), (0, n_pad)))
    Np = N + n_pad
    groups = mod[0].shape[0] if mod is not None else (ln[1].shape[0] if ln is not None else 1)
    rows = M // groups
    tm = _pick(rows, (1024, 512, 256, 128, 64, 32, 16, 8))
    tn = Np if ln is not None else _lane_tile(Np, 1408)
    tk = K if (mod is not None or ln is not None) else _pick(K, (1024, 512, 256, 128))
    nk = K // tk
    per_group = lambda width: pl.BlockSpec((1, 1, width), lambda i, j, k: ((i * tm) // rows, 0, 0))
    specs = [pl.BlockSpec((tm, tk), lambda i, j, k: (i, k)),
             pl.BlockSpec((tk, tn), lambda i, j, k: (k, j))]
    args = [a, wb]
    if mod is not None:
        specs += [per_group(K), per_group(K)]
        args += list(mod)
    if ln is not None:
        assert n_pad == 0
        row_vec = pl.BlockSpec((1, N), lambda i, j, k: (0, 0))
        specs += [pl.BlockSpec((tm, N), lambda i, j, k: (i, 0)), per_group(N), row_vec, row_vec]
        args += list(ln)
    return pl.pallas_call(
        functools.partial(_mm_kernel, nk=nk, mod=mod is not None, ln=ln is not None),
        grid=(M // tm, Np // tn, nk),
        in_specs=specs,
        out_specs=pl.BlockSpec((tm, tn), lambda i, j, k: (i, j)),
        out_shape=jax.ShapeDtypeStruct((M, Np), F32),
        compiler_params=_cparams(("parallel", "parallel", "arbitrary")),
        name="mm",
    )(*args)


def _mm3(u, w, mod=None):
    B, L, K = u.shape
    return _mm(u.reshape(B * L, K), w, mod=mod).reshape(B, L, -1)


def _ffn_kernel(*refs, nf, block, row_scale):
    refs = list(refs)
    u_ref, wg_ref, wu_ref, wd_ref = refs[1:5]
    o_ref, acc_ref = refs[-2:]
    extra = refs[5:-2]
    f = pl.program_id(1)

    @pl.when(f == 0)
    def _():
        acc_ref[...] = jnp.zeros_like(acc_ref)

    u = u_ref[...]
    if block:
        u = u * (1.0 + extra[0][0]) + extra[1][0]
    u = u.astype(BF16)
    g = jnp.dot(u, wg_ref[0], preferred_element_type=F32)
    up = jnp.dot(u, wu_ref[0], preferred_element_type=F32)
    act = (g * jax.nn.sigmoid(g) * up).astype(BF16)
    acc_ref[...] += jnp.dot(act, wd_ref[0], preferred_element_type=F32)

    @pl.when(f == nf - 1)
    def _():
        y = acc_ref[...]
        if row_scale:
            y = y * extra[0][...]
        if block:
            y = _ln_rows(DEEPNORM_ALPHA * u_ref[...] + extra[2][0] * y, extra[3][...], extra[4][...])
        o_ref[...] = y.astype(o_ref.dtype)


def _ffn(u, w_gu, w_down, tile_expert, tm, block=None, row_scale=None, out_dtype=F32):
    M, D = u.shape
    F = w_down.shape[1]
    tf = _pick(F, (1792, 1408, 512, 256, 128))
    nf = F // tf
    specs = [pl.BlockSpec((tm, D), lambda i, f, te: (i, 0)),
             pl.BlockSpec((1, D, tf), lambda i, f, te: (te[i], 0, f)),
             pl.BlockSpec((1, D, tf), lambda i, f, te: (te[i], 0, f + nf)),
             pl.BlockSpec((1, tf, D), lambda i, f, te: (te[i], f, 0))]
    args = [tile_expert, u, w_gu, w_gu, w_down]
    if block is not None:
        rows = M // block[0].shape[0]
        assert rows % tm == 0
        per_group = pl.BlockSpec((1, 1, D), lambda i, f, te: ((i * tm) // rows, 0, 0))
        row_vec = pl.BlockSpec((1, D), lambda i, f, te: (0, 0))
        specs += [per_group, per_group, per_group, row_vec, row_vec]
        args += list(block)
    if row_scale is not None:
        specs.append(pl.BlockSpec((tm, 1), lambda i, f, te: (i, 0)))
        args.append(row_scale)
    grid_spec = pltpu.PrefetchScalarGridSpec(
        num_scalar_prefetch=1,
        grid=(M // tm, nf),
        in_specs=specs,
        out_specs=pl.BlockSpec((tm, D), lambda i, f, te: (i, 0)),
        scratch_shapes=[pltpu.VMEM((tm, D), F32)])
    return pl.pallas_call(
        functools.partial(_ffn_kernel, nf=nf, block=block is not None, row_scale=row_scale is not None),
        grid_spec=grid_spec,
        out_shape=jax.ShapeDtypeStruct((M, D), out_dtype),
        compiler_params=_cparams(("parallel", "arbitrary")),
        name="ffn",
    )(*args)


def _grid_sincos(rows, dtype):
    r, col = jnp.meshgrid(jnp.arange(rows, dtype=F32), jnp.arange(GRID_W, dtype=F32), indexing='ij')
    n_freq = D_MODEL // 4
    omega = 1.0 / (10000.0 ** (jnp.arange(n_freq, dtype=F32) / n_freq))

    def emb(p):
        ang = p.reshape(-1)[:, None] * omega[None, :]
        return jnp.concatenate([jnp.sin(ang), jnp.cos(ang)], -1)

    return jnp.concatenate([emb(r), emb(col)], -1).astype(dtype)


def _chunk_masks(rev):
    row = lax.broadcasted_iota(jnp.int32, (CHUNK, CHUNK), 0)
    col = lax.broadcasted_iota(jnp.int32, (CHUNK, CHUNK), 1)
    if rev:
        return row, col, col >= row, col > row
    return row, col, col <= row, col < row


def _bdot(a, b, dims=None):
    a = a.astype(BF16)
    b = b.astype(BF16)
    if dims is None:
        return jnp.dot(a, b, preferred_element_type=F32)
    return lax.dot_general(a, b, dims, preferred_element_type=F32)


def _scan_specs(nc, rev, widths_and_blocks):
    def cidx(j):
        return nc - 1 - j if rev else j
    return [pl.BlockSpec((1, CHUNK, w), functools.partial(lambda b, j, cb: (b, cidx(j), cb), cb=cb))
            for w, cb in widths_and_blocks]


def _head_readout(o_h, norm_g, gate):
    inv = lax.rsqrt(jnp.mean(o_h * o_h, axis=-1, keepdims=True) + NORM_EPS)
    return o_h * inv * norm_g * (gate * jax.nn.sigmoid(gate))


def _gla_kernel(*refs, rev, nc, with_prev, readout):
    refs = list(refs)
    q_ref, k_ref, v_ref, t_ref, w2_ref, gb_ref, s0_ref = refs[:7]
    o_ref, sfin_ref, s_scr = refs[-3:]
    extra = refs[7:-3]
    prev_ref = extra.pop(0) if with_prev else None
    gate_ref, ng_ref = extra if readout else (None, None)
    j = pl.program_id(1)

    @pl.when(j == 0)
    def _():
        s_scr[...] = s0_ref[0]

    _, _, incl, _ = _chunk_masks(rev)
    z = _bdot(t_ref[0], w2_ref[...]) + gb_ref[...]
    log_a = (jnp.minimum(z, 0.0) - jnp.log(1.0 + jnp.exp(-jnp.abs(z)))) * (1.0 / GLA_TAU)
    b = jnp.dot(incl.astype(F32), log_a, precision=HIGHEST, preferred_element_type=F32)
    last = 0 if rev else CHUNK - 1
    heads = range(GLA_HEADS)
    ks = [slice(h * GLA_DK, (h + 1) * GLA_DK) for h in heads]
    vs = [slice(h * GLA_DV, (h + 1) * GLA_DV) for h in heads]
    bhs = [b[:, ks[h]] for h in heads]
    b_lasts = [bh[last:last + 1, :] for bh in bhs]
    kfs = [k_ref[0, :, ks[h]] for h in heads]
    vbs = [v_ref[0, :, vs[h]].astype(BF16) for h in heads]
    sts = [s_scr[h] for h in heads]
    q_decs = [(q_ref[0, :, ks[h]] * (GLA_DK ** -0.5) * jnp.exp(bhs[h])).astype(BF16) for h in heads]
    k_invs = [(kfs[h] * jnp.exp(-bhs[h])).astype(BF16) for h in heads]
    scores = [jnp.where(incl, _bdot(q_decs[h], k_invs[h], _NT), 0.0) for h in heads]
    inter = [_bdot(q_decs[h], sts[h], _NT) for h in heads]
    intra = [_bdot(scores[h], vbs[h]) for h in heads]
    kv = [_bdot(vbs[h], kfs[h] * jnp.exp(b_lasts[h] - bhs[h]), _TN) for h in heads]
    for h in heads:
        s_scr[h] = sts[h] * jnp.exp(b_lasts[h]) + kv[h]
        o_h = intra[h] + inter[h]
        if prev_ref is not None:
            o_h = o_h + prev_ref[0, :, vs[h]]
        if readout:
            o_h = _head_readout(o_h, ng_ref[...], gate_ref[0, :, vs[h]])
        o_ref[0, :, vs[h]] = o_h.astype(o_ref.dtype)

    @pl.when(j == nc - 1)
    def _():
        sfin_ref[0] = s_scr[...]


def _gla_scan(proj, w2, gb, s0, prev, rev, norm_g=None):
    B, L, _ = proj.shape
    nc = L // CHUNK
    with_prev = prev is not None
    readout = norm_g is not None
    gate_cb = (2 * GLA_KEY + 2 * GLA_VAL) // LANES
    specs = _scan_specs(nc, rev, [(GLA_KEY, 0), (GLA_KEY, 1), (GLA_VAL, 1), (LANES, gate_cb)])
    specs += [pl.BlockSpec((LANES, GLA_KEY), lambda b, j: (0, 0)),
              pl.BlockSpec((1, GLA_KEY), lambda b, j: (0, 0)),
              pl.BlockSpec((1, GLA_HEADS, GLA_DV, GLA_DK), lambda b, j: (b, 0, 0, 0))]
    args = [proj, proj, proj, proj, w2, gb, s0]
    o_spec = _scan_specs(nc, rev, [(GLA_VAL, 0)])[0]
    if with_prev:
        specs.append(o_spec)
        args.append(prev)
    if readout:
        specs += [_scan_specs(nc, rev, [(GLA_VAL, 2)])[0], pl.BlockSpec((1, GLA_DV), lambda b, j: (0, 0))]
        args += [proj, norm_g.reshape(1, GLA_DV)]
    return pl.pallas_call(
        functools.partial(_gla_kernel, rev=rev, nc=nc, with_prev=with_prev, readout=readout),
        grid=(B, nc),
        in_specs=specs,
        out_specs=[o_spec, pl.BlockSpec((1, GLA_HEADS, GLA_DV, GLA_DK), lambda b, j: (b, 0, 0, 0))],
        out_shape=[jax.ShapeDtypeStruct((B, L, GLA_VAL), BF16 if readout else F32),
                   jax.ShapeDtypeStruct((B, GLA_HEADS, GLA_DV, GLA_DK), F32)],
        scratch_shapes=[pltpu.VMEM((GLA_HEADS, GLA_DV, GLA_DK), F32)],
        compiler_params=_cparams(("parallel", "arbitrary")),
        name="gla_scan_rev" if rev else "gla_scan_fwd",
    )(*args)


def _gla_mixer(x_c, mod_c, x_x, mod_x, w_in, gate_w1, gate_w2, gate_b, norm_g):
    w_ext = jnp.concatenate([w_in, gate_w1[0], gate_w1[1]], axis=1)
    w2 = []
    for d in range(2):
        w2.append(jnp.zeros((LANES, GLA_KEY), F32).at[d * GLA_RANK:(d + 1) * GLA_RANK].set(gate_w2[d]).astype(BF16))
    gb = gate_b[:, None, :]

    def bidir(proj, s0f, s0b):
        o, s_f = _gla_scan(proj, w2[0], gb[0], s0f, None, False)
        y, s_b = _gla_scan(proj, w2[1], gb[1], s0b, o, True, norm_g=norm_g)
        return y, s_f, s_b

    s0 = jnp.zeros((x_c.shape[0], GLA_HEADS, GLA_DV, GLA_DK), F32)
    y_c, s_f, s_b = bidir(_mm3(x_c, w_ext, mod=mod_c), s0, s0)
    y_x, _, _ = bidir(_mm3(x_x, w_ext, mod=mod_x), s_f, s_b)
    return y_c, y_x


def _dn_prep_kernel(x_ref, w_ref, o_ref, *, L):
    cb = pl.program_id(1)
    y = _dwconv3_rows(x_ref[0], w_ref[...])
    y = y * jax.nn.sigmoid(y)
    inv = lax.rsqrt(jnp.sum(y * y, axis=1, keepdims=True) + NORM_EPS)
    q_scale = jnp.where(cb < DN_HEADS, DN_DK ** -0.5, 1.0)
    fac = jnp.where(cb < 2 * DN_HEADS, inv * q_scale, 1.0)
    o_ref[0] = (y * fac).astype(BF16)


def _dn_prep(proj, conv_w):
    B, L, _ = proj.shape
    ncb = (2 * DN_KEY + DN_VAL) // LANES
    return pl.pallas_call(
        functools.partial(_dn_prep_kernel, L=L),
        grid=(B, ncb),
        in_specs=[pl.BlockSpec((1, L, LANES), lambda b, c: (b, 0, c)),
                  pl.BlockSpec((conv_w.shape[0], LANES), lambda b, c: (0, c))],
        out_specs=pl.BlockSpec((1, L, LANES), lambda b, c: (b, 0, c)),
        out_shape=jax.ShapeDtypeStruct((B, L, ncb * LANES), BF16),
        compiler_params=_cparams(("parallel", "parallel")),
        name="dn_prep",
    )(proj, conv_w)


DN_GROUP = 4
DN_ROWS = DN_GROUP * CHUNK


DN_CPS = 2
DN_LEVELS = CHUNK.bit_length() - 1
M_INCL, M_STRICT, M_EYE, M_LEVEL0 = 0, 1, 2, 3


def _dn_masks(rev):
    r = jnp.arange(DN_ROWS, dtype=jnp.int32)[:, None]
    c = jnp.arange(DN_ROWS, dtype=jnp.int32)[None, :]
    same_head = (r // CHUNK) == (c // CHUNK)
    rt, ct = r % CHUNK, c % CHUNK
    out = [same_head & ((ct >= rt) if rev else (ct <= rt)), same_head & ((ct > rt) if rev else (ct < rt)), r == c]
    for lv in range(DN_LEVELS):
        s = 1 << lv
        r_hi, c_hi = (r // s) % 2 == 1, (c // s) % 2 == 1
        out.append(((r // (2 * s)) == (c // (2 * s))) & ((c_hi & ~r_hi) if rev else (r_hi & ~c_hi)))
    return jnp.stack(out).astype(F32)


def _dn_chunk_kernel(q_ref, k_ref, v_ref, ba_ref, par_ref, m_ref, p_ref, qm_ref, g_ref, r_ref, oi_ref, *, rev):
    prow = lax.broadcasted_iota(jnp.int32, (2 * CHUNK, CHUNK), 0)
    pcol = lax.broadcasted_iota(jnp.int32, (2 * CHUNK, CHUNK), 1)
    tri_pad = (((pcol >= prow) if rev else (pcol <= prow)) & (prow < CHUNK)).astype(F32)
    d = 1 if rev else 0
    last = 0 if rev else CHUNK - 1
    incl = m_ref[M_INCL]
    eye = m_ref[M_EYE]
    items = []
    for ci in range(DN_CPS):
        tr = slice(ci * CHUNK, (ci + 1) * CHUNK)
        ba = ba_ref[0, tr, :]
        beta_all = jax.nn.sigmoid(ba)
        sp_in = ba + par_ref[1:2, :]
        softplus = jnp.maximum(sp_in, 0.0) + jnp.log(1.0 + jnp.exp(-jnp.abs(sp_in)))
        g_all = par_ref[0:1, :] * softplus
        gc_pad = jnp.dot(tri_pad, g_all, precision=HIGHEST, preferred_element_type=F32)
        gc = gc_pad[:CHUNK]
        gc_t = gc_pad.T
        for grp in range(DN_HEADS // DN_GROUP):
            heads = range(grp * DN_GROUP, (grp + 1) * DN_GROUP)
            gls = [2 * DN_HEADS + d * DN_HEADS + h for h in heads]
            qs = jnp.concatenate([q_ref[0, tr, h * DN_DK:(h + 1) * DN_DK] for h in heads], axis=0)
            ks = jnp.concatenate([k_ref[0, tr, h * DN_DK:(h + 1) * DN_DK] for h in heads], axis=0)
            vs = jnp.concatenate([v_ref[0, tr, h * DN_DV:(h + 1) * DN_DV] for h in heads], axis=0)
            beta = jnp.concatenate([beta_all[:, d * DN_HEADS + h:d * DN_HEADS + h + 1] for h in heads], axis=0)
            gcol = jnp.concatenate([gc[:, gl:gl + 1] for gl in gls], axis=0)
            glast = jnp.concatenate([jnp.broadcast_to(gc[last:last + 1, gl:gl + 1], (CHUNK, 1)) for gl in gls],
                                    axis=0)
            rows = [gc_t[gl:gl + 1, :] for gl in gls]
            grow = jnp.concatenate([rows[0] + pltpu.roll(rows[1], CHUNK, 1),
                                    rows[2] + pltpu.roll(rows[3], CHUNK, 1)], axis=1)
            items.append(dict(ci=ci, tr=tr, heads=heads, qs=qs, ks=ks, vs=vs, beta=beta, gcol=gcol, glast=glast,
                              grow=grow, gfin=[gc[last:last + 1, gl:gl + 1] for gl in gls]))
    dms = [jnp.exp(jnp.minimum(it["gcol"] - it["grow"], 0.0)) * incl for it in items]
    kfs = [it["ks"].astype(F32) for it in items]
    kbetas = [kf * it["beta"] for kf, it in zip(kfs, items)]
    a_s = [_bdot(kb, it["ks"], _NT) * (dm * m_ref[M_STRICT]) for kb, it, dm in zip(kbetas, items, dms)]
    ts = [eye - a * m_ref[M_LEVEL0] for a in a_s]
    for lv in range(1, DN_LEVELS):
        mts = [_bdot(a * m_ref[M_LEVEL0 + lv], t) for a, t in zip(a_s, ts)]
        ts = [t - _bdot(t, mt) for t, mt in zip(ts, mts)]
    rhs = [jnp.concatenate([it["vs"].astype(F32) * it["beta"], kb * jnp.exp(it["gcol"])], axis=1)
           for it, kb in zip(items, kbetas)]
    sols = [_bdot(t, r) for t, r in zip(ts, rhs)]
    qks = [_bdot(it["qs"], it["ks"], _NT) * dm for it, dm in zip(items, dms)]
    qsols = [_bdot(qk, sol) for qk, sol in zip(qks, sols)]
    for it, kf, sol, qsol in zip(items, kfs, sols, qsols):
        ci, tr = it["ci"], it["tr"]
        k_dec = kf * jnp.exp(it["glast"] - it["gcol"])
        r_all = it["qs"].astype(F32) * jnp.exp(it["gcol"]) - qsol[:, DN_DV:]
        for i, h in enumerate(it["heads"]):
            rs = slice(i * CHUNK, (i + 1) * CHUNK)
            hs = slice(h * DN_DV, (h + 1) * DN_DV)
            pq = _bdot(k_dec[rs], sol[rs], _TN)
            qm_ref[0, ci, h] = pq[:, :DN_DV]
            p_ref[0, ci, h] = (-pq[:, DN_DV:]).astype(BF16)
            g_ref[0, ci, h:h + 1, :] = jnp.broadcast_to(jnp.exp(it["gfin"][i]), (1, DN_DV))
            r_ref[0, tr, hs] = r_all[rs].astype(BF16)
            oi_ref[0, tr, hs] = qsol[rs, :DN_DV]


def _dn_chunk(qkv, proj, par, rev):
    B, L, _ = qkv.shape
    nc = L // CHUNK
    assert nc % DN_CPS == 0
    ba_cb = (2 * DN_KEY + 2 * DN_VAL) // LANES
    tok = lambda w, cb: pl.BlockSpec((1, DN_CPS * CHUNK, w), lambda b, j: (b, j, cb))
    mat = pl.BlockSpec((1, DN_CPS, DN_HEADS, DN_DK, DN_DV), lambda b, j: (b, j, 0, 0, 0))
    masks = _dn_masks(rev)
    return pl.pallas_call(
        functools.partial(_dn_chunk_kernel, rev=rev),
        grid=(B, nc // DN_CPS),
        in_specs=[tok(DN_KEY, 0), tok(DN_KEY, 1), tok(DN_VAL, 2), tok(LANES, ba_cb),
                  pl.BlockSpec((2, LANES), lambda b, j: (0, 0)),
                  pl.BlockSpec(masks.shape, lambda b, j: (0, 0, 0), pipeline_mode=pl.Buffered(1))],
        out_specs=[mat, mat, pl.BlockSpec((1, DN_CPS, DN_HEADS, DN_DV), lambda b, j: (b, j, 0, 0)),
                   tok(DN_KEY, 0), tok(DN_VAL, 0)],
        out_shape=[jax.ShapeDtypeStruct((B, nc, DN_HEADS, DN_DK, DN_DV), BF16),
                   jax.ShapeDtypeStruct((B, nc, DN_HEADS, DN_DK, DN_DV), F32),
                   jax.ShapeDtypeStruct((B, nc, DN_HEADS, DN_DV), F32),
                   jax.ShapeDtypeStruct((B, L, DN_KEY), BF16),
                   jax.ShapeDtypeStruct((B, L, DN_VAL), F32)],
        compiler_params=_cparams(("parallel", "parallel")),
        name="dn_chunk_rev" if rev else "dn_chunk_fwd",
    )(qkv, qkv, qkv, proj, par, masks)


def _dn_seq_kernel(*refs, nc, with_prev, readout):
    refs = list(refs)
    p_ref, qm_ref, g_ref, r_ref, oi_ref, s0_ref = refs[:6]
    o_ref, sfin_ref, s_scr = refs[-3:]
    extra = refs[6:-3]
    prev_ref = extra.pop(0) if with_prev else None
    gate_ref, ng_ref = extra if readout else (None, None)
    j = pl.program_id(1)

    @pl.when(j == 0)
    def _():
        s_scr[...] = s0_ref[0]

    states = [s_scr[h] for h in range(DN_HEADS)]
    xs = []
    for h in range(DN_HEADS):
        lhs = jnp.concatenate([p_ref[0, 0, h], r_ref[0, :, h * DN_DK:(h + 1) * DN_DK]], axis=0)
        xs.append(_bdot(lhs, states[h]))
    for h in range(DN_HEADS):
        hs = slice(h * DN_DV, (h + 1) * DN_DV)
        s_scr[h] = g_ref[0, 0, h:h + 1, :] * states[h] + xs[h][:DN_DK] + qm_ref[0, 0, h]
        o_h = xs[h][DN_DK:] + oi_ref[0, :, hs]
        if prev_ref is not None:
            o_h = o_h + prev_ref[0, :, hs]
        if readout:
            o_h = _head_readout(o_h, ng_ref[...], gate_ref[0, :, hs])
        o_ref[0, :, hs] = o_h.astype(o_ref.dtype)

    @pl.when(j == nc - 1)
    def _():
        sfin_ref[0] = s_scr[...]


def _dn_seq(chunk_terms, s0, prev, rev, readout=None):
    p, qm, g, r, oi = chunk_terms
    B, L, _ = r.shape
    nc = L // CHUNK
    with_prev = prev is not None

    def cidx(j):
        return nc - 1 - j if rev else j

    mat = pl.BlockSpec((1, 1, DN_HEADS, DN_DK, DN_DV), lambda b, j: (b, cidx(j), 0, 0, 0))
    tok = pl.BlockSpec((1, CHUNK, DN_VAL), lambda b, j: (b, cidx(j), 0))
    state = pl.BlockSpec((1, DN_HEADS, DN_DK, DN_DV), lambda b, j: (b, 0, 0, 0))
    specs = [mat, mat, pl.BlockSpec((1, 1, DN_HEADS, DN_DV), lambda b, j: (b, cidx(j), 0, 0)), tok, tok, state]
    args = [p, qm, g, r, oi, s0]
    if with_prev:
        specs.append(tok)
        args.append(prev)
    if readout is not None:
        z_cb = (2 * DN_KEY + DN_VAL) // DN_VAL
        specs += [pl.BlockSpec((1, CHUNK, DN_VAL), lambda b, j: (b, cidx(j), z_cb)),
                  pl.BlockSpec((1, DN_DV), lambda b, j: (0, 0))]
        args += [readout[0], readout[1].reshape(1, DN_DV)]
    return pl.pallas_call(
        functools.partial(_dn_seq_kernel, nc=nc, with_prev=with_prev, readout=readout is not None),
        grid=(B, nc),
        in_specs=specs,
        out_specs=[tok, state],
        out_shape=[jax.ShapeDtypeStruct((B, L, DN_VAL), BF16 if readout is not None else F32),
                   jax.ShapeDtypeStruct((B, DN_HEADS, DN_DK, DN_DV), F32)],
        scratch_shapes=[pltpu.VMEM((DN_HEADS, DN_DK, DN_DV), F32)],
        compiler_params=_cparams(("parallel", "arbitrary")),
        name="dn_seq_rev" if rev else "dn_seq_fwd",
    )(*args)


def _delta_mixer(x_c, mod_c, x_x, mod_x, w_in, conv_w, A_log, dt_bias, norm_g):
    a_lanes = slice(2 * DN_HEADS, 4 * DN_HEADS)
    par = jnp.zeros((2, LANES), F32)
    par = par.at[0, a_lanes].set(-jnp.exp(A_log).reshape(-1)).at[1, a_lanes].set(dt_bias.reshape(-1))

    def bidir(proj, s0f, s0b):
        qkv = _dn_prep(proj, conv_w)
        o, s_f = _dn_seq(_dn_chunk(qkv, proj, par, False), s0f, None, False)
        y, s_b = _dn_seq(_dn_chunk(qkv, proj, par, True), s0b, o, True, readout=(proj, norm_g))
        return y, s_f, s_b

    s0 = jnp.zeros((x_c.shape[0], DN_HEADS, DN_DK, DN_DV), F32)
    y_c, s_f, s_b = bidir(_mm3(x_c, w_in, mod=mod_c), s0, s0)
    y_x, _, _ = bidir(_mm3(x_x, w_in, mod=mod_x), s_f, s_b)
    return y_c, y_x


def _hyena_filter_spectrum(L, pe_w1, pe_b1, freq, pe_w2, pe_b2, pe_w3, decay):
    t = jnp.arange(L, dtype=F32)
    tn = t / L
    ang = (2.0 * math.pi / L) * t[:, None] * jnp.arange(1, HY_BANDS + 1, dtype=F32)[None, :]
    feats = jnp.concatenate([tn[:, None], jnp.cos(ang), jnp.sin(ang)], -1)
    h = jnp.sin(freq * (feats @ pe_w1 + pe_b1))
    h = jnp.sin(freq * (h @ pe_w2 + pe_b2))
    h = (h @ pe_w3) * (jnp.exp(-tn[:, None] * decay) + HY_SHIFT)
    h = h.reshape(L, HY_ORDER, HY_SIDES, D_MODEL)
    h_circ = jnp.concatenate([h[:, :, 0], jnp.zeros((1, HY_ORDER, D_MODEL), F32), h[:0:-1, :, 1]], axis=0)
    return jnp.fft.rfft(h_circ, axis=0)


HY_TD = 256
HY_TK = 512


def _dwconv3_rows(x, w):
    L = x.shape[0]
    rowi = lax.broadcasted_iota(jnp.int32, x.shape, 0)
    x_prev = jnp.where(rowi == 0, 0.0, pltpu.roll(x, 1, 0))
    x_next = jnp.where(rowi == L - 1, 0.0, pltpu.roll(x, L - 1, 0))
    return w[0:1] * x_prev + w[1:2] * x + w[2:3] * x_next


def _hy_tables(L):
    n = 2 * L
    k = jnp.arange(L, dtype=jnp.int32)
    ang = (2.0 * math.pi / n) * ((k[:, None] * k[None, :]) % n).astype(F32)
    sign = 1.0 - 2.0 * (k % 2).astype(F32)
    ws = (-jnp.sin(ang)).at[0, :].set(sign)
    return jnp.cos(ang).astype(BF16), ws.astype(BF16), ws.T.astype(BF16)


def _hy_filter(L, filt):
    spec = _hyena_filter_spectrum(L, *filt)
    n = 2 * L
    scale = jnp.full((L, 1, 1), 2.0 / n, F32).at[0].set(1.0 / n)
    hr = jnp.real(spec[:L]) * scale
    hi = (jnp.imag(spec[:L]) * scale).at[0].set(0.0)
    hr2 = hr.at[0].set(jnp.real(spec[L]) / n)
    return tuple(jnp.transpose(a, (1, 0, 2)) for a in (hr, hi, hr2))


def _hy_fwd_kernel(x_ref, cw_ref, c_ref, ws_ref, hr_ref, hi_ref, hr2_ref, yr_ref, yi_ref, *, L, conv):
    x = x_ref[0]
    vb = (_dwconv3_rows(x, cw_ref[...]) if conv else x).astype(BF16)
    tk = min(HY_TK, L)
    for kb in range(L // tk):
        rows = slice(kb * tk, (kb + 1) * tk)
        ur = jnp.dot(c_ref[rows, :], vb, preferred_element_type=F32)
        ui = jnp.dot(ws_ref[rows, :], vb, preferred_element_type=F32)
        hr = hr_ref[rows, :]
        hi = hi_ref[rows, :]
        yr_ref[0, rows, :] = (ur * hr - ui * hi).astype(BF16)
        yi_ref[0, rows, :] = (ur * hi + ui * hr2_ref[rows, :]).astype(BF16)


def _hy_inv_kernel(yr_ref, yi_ref, c_ref, wst_ref, g_ref, gw_ref, v_ref, vw_ref, sk_ref, o_ref, *, L, conv_v):
    gate = _dwconv3_rows(g_ref[0], gw_ref[...])
    v = v_ref[0]
    if conv_v:
        v = _dwconv3_rows(v, vw_ref[...])
    yr = yr_ref[0]
    yi = yi_ref[0]
    tk = min(HY_TK, L)
    for tb in range(L // tk):
        rows = slice(tb * tk, (tb + 1) * tk)
        y = (jnp.dot(c_ref[rows, :], yr, preferred_element_type=F32)
             + jnp.dot(wst_ref[rows, :], yi, preferred_element_type=F32))
        o_ref[0, rows, :] = (gate[rows, :] * (y + sk_ref[...] * v[rows, :])).astype(o_ref.dtype)


def _hy_specs(L, td):
    table = pl.BlockSpec((L, L), lambda d, b: (0, 0), pipeline_mode=pl.Buffered(1))
    col = lambda off: pl.BlockSpec((1, L, td), functools.partial(lambda d, b, off: (b, 0, off + d), off=off))
    par = lambda r: pl.BlockSpec((r, td), lambda d, b: (0, d))
    return table, col, par


def _hy_fwd(x, col_off, conv_w, c, ws, hr, hi, hr2):
    B, L, _ = x.shape
    D = hr.shape[1]
    td = HY_TD
    table, col, par = _hy_specs(L, td)
    conv = conv_w is not None
    if not conv:
        conv_w = jnp.zeros((3, D), F32)
    out = jax.ShapeDtypeStruct((B, L, D), BF16)
    return pl.pallas_call(
        functools.partial(_hy_fwd_kernel, L=L, conv=conv),
        grid=(D // td, B),
        in_specs=[col(col_off), par(3), table, table, par(L), par(L), par(L)],
        out_specs=[col(0), col(0)],
        out_shape=[out, out],
        compiler_params=_cparams(("parallel", "parallel")),
        name="hy_fwd",
    )(x, conv_w, c, ws, hr, hi, hr2)


def _hy_inv(yr, yi, c, wst, proj, gate_off, gate_w, sig, sig_off, sig_w, skip, out_dtype=F32):
    B, L, D = yr.shape
    td = HY_TD
    table, col, par = _hy_specs(L, td)
    conv_v = sig_w is not None
    if not conv_v:
        sig_w = jnp.zeros((3, D), F32)
    return pl.pallas_call(
        functools.partial(_hy_inv_kernel, L=L, conv_v=conv_v),
        grid=(D // td, B),
        in_specs=[col(0), col(0), table, table, col(gate_off), par(3), col(sig_off), par(3), par(1)],
        out_specs=col(0),
        out_shape=jax.ShapeDtypeStruct((B, L, D), out_dtype),
        compiler_params=_cparams(("parallel", "parallel")),
        name="hy_inv",
    )(yr, yi, c, wst, proj, gate_w, sig, sig_w, skip)


def _hyena_seq(x, mod, w_in, conv_w, filt, skip):
    B, L, D = x.shape
    assert conv_w.shape[0] == 3 and D % HY_TD == 0
    nd = D // HY_TD
    proj = _mm3(x, w_in, mod=mod)
    cw = [conv_w[:, i * D:(i + 1) * D] for i in range(3)]
    c, ws, wst = _hy_tables(L)
    hr, hi, hr2 = _hy_filter(L, filt)
    yr, yi = _hy_fwd(proj, 2 * nd, cw[2], c, ws, hr[0], hi[0], hr2[0])
    z = _hy_inv(yr, yi, c, wst, proj, 0, cw[0], proj, 2 * nd, cw[2], skip[0:1])
    yr, yi = _hy_fwd(z, 0, None, c, ws, hr[1], hi[1], hr2[1])
    return _hy_inv(yr, yi, c, wst, proj, nd, cw[1], z, 0, None, skip[1:2], out_dtype=BF16)


def _dense_block(x, mod, gate, ln_g, ln_b, w_gu, w_down):
    B, L, D = x.shape
    M = B * L
    tm = _pick(M // mod[0].shape[0], (512, 256, 128))
    te = jnp.zeros((M // tm,), jnp.int32)
    block = (mod[0], mod[1], gate, ln_g, ln_b)
    return _ffn(x.reshape(M, D), w_gu.astype(BF16)[None], w_down.astype(BF16)[None], te, tm,
                block=block).reshape(B, L, D)


def _moe(h, router, w_gu, w_down):
    B, L, D = h.shape
    M = B * L
    tm = MOE_TILE
    hf = h.reshape(M, D)
    logits = jnp.dot(hf, router, precision=HIGHEST)
    top_val, top_idx = lax.top_k(logits, TOP_K)
    weights = jax.nn.softmax(top_val, axis=-1)
    e_flat = top_idx.reshape(-1).astype(jnp.int32)
    onehot = (e_flat[:, None] == jnp.arange(N_EXPERTS, dtype=jnp.int32)[None, :]).astype(jnp.int32)
    rank = jnp.sum((jnp.cumsum(onehot, axis=0) - onehot) * onehot, axis=1)
    counts = jnp.sum(onehot, axis=0)
    padded = ((counts + tm - 1) // tm) * tm
    ends = jnp.cumsum(padded)
    pos = (ends - padded)[e_flat] + rank
    P = M * TOP_K + N_EXPERTS * tm
    slot = jnp.zeros((P,), jnp.int32).at[pos].set(jnp.arange(M * TOP_K, dtype=jnp.int32))
    tile_expert = jnp.minimum(jnp.searchsorted(ends, jnp.arange(P // tm, dtype=jnp.int32) * tm, side='right'),
                              N_EXPERTS - 1).astype(jnp.int32)
    xs = hf.astype(BF16)[slot // TOP_K]
    row_w = weights.reshape(-1)[slot][:, None]
    ys = _ffn(xs, w_gu.astype(BF16), w_down.astype(BF16), tile_expert, tm, row_scale=row_w, out_dtype=BF16)
    pos2 = pos.reshape(M, TOP_K)
    y = ys[pos2[:, 0]].astype(F32) + ys[pos2[:, 1]].astype(F32)
    return y.reshape(B, L, D)


def _layer_norm(x, g, b):
    xc = x - jnp.mean(x, -1, keepdims=True)
    var = jnp.mean(xc * xc, -1, keepdims=True)
    return xc * lax.rsqrt(var + LN_EPS) * g + b


def kernel(x, c, ctx, c_ctx, ada_w, ada_b, ln_g, ln_b,
           gla_w_in, gla_gate_w1, gla_gate_w2, gla_gate_b, gla_norm_g, gla_w_out,
           dn_w_in, dn_conv, dn_A_log, dn_dt_bias, dn_norm_g, dn_w_out,
           hy_w_in, hy_conv, hy_pe_w1, hy_pe_b1, hy_freq, hy_pe_w2, hy_pe_b2, hy_pe_w3,
           hy_decay, hy_skip, hy_w_out,
           ffn_w_gu, ffn_w_down, moe_router, moe_w_gu, moe_w_down):
    B, L, D = x.shape
    x = x + _grid_sincos(L // GRID_W, x.dtype)[None]
    xc = ctx
    cond_x = jax.nn.silu(c)
    cond_c = jax.nn.silu(c_ctx)[None, :]

    def project_out(y, w_out, res, gate, g, b):
        Bn, Ln, _ = res.shape
        return _mm(y.reshape(Bn * Ln, -1), w_out, ln=(res.reshape(Bn * Ln, D), gate, g, b)).reshape(Bn, Ln, D)

    for i in range(DEPTH):
        last = i == DEPTH - 1
        kind = i % N_MIXERS
        j = i // N_MIXERS
        mx = (jnp.dot(cond_x, ada_w[i], precision=HIGHEST) + ada_b[i]).reshape(B, 6, 1, D)
        mc = (jnp.dot(cond_c, ada_w[i], precision=HIGHEST) + ada_b[i]).reshape(1, 6, 1, D)
        sh1, sc1, g1, sh2, sc2, g2 = (mx[:, t] for t in range(6))
        csh1, csc1, cg1, csh2, csc2, cg2 = (mc[:, t] for t in range(6))
        ln1 = (ln_g[i, 0][None, :], ln_b[i, 0][None, :])
        ln2 = (ln_g[i, 1][None, :], ln_b[i, 1][None, :])
        if kind == 0:
            y_c, y_x = _gla_mixer(xc, (csc1, csh1), x, (sc1, sh1), gla_w_in[j], gla_gate_w1[j], gla_gate_w2[j],
                                  gla_gate_b[j], gla_norm_g[j])
            w_out = gla_w_out[j]
        elif kind == 1:
            y_c, y_x = _delta_mixer(xc, (csc1, csh1), x, (sc1, sh1), dn_w_in[j], dn_conv[j], dn_A_log[j],
                                    dn_dt_bias[j], dn_norm_g[j])
            w_out = dn_w_out[j]
        else:
            filt = (hy_pe_w1[j], hy_pe_b1[j], hy_freq[j], hy_pe_w2[j], hy_pe_b2[j], hy_pe_w3[j], hy_decay[j])
            y_x = _hyena_seq(x, (sc1, sh1), hy_w_in[j], hy_conv[j], filt, hy_skip[j])
            y_c = None if last else _hyena_seq(xc, (csc1, csh1), hy_w_in[j], hy_conv[j], filt, hy_skip[j])
            w_out = hy_w_out[j]
        x = project_out(y_x, w_out, x, g1, *ln1)
        if not last:
            xc = project_out(y_c, w_out, xc, cg1, *ln1)

        if i % 2 == 0:
            x = _dense_block(x, (sc2, sh2), g2, *ln2, ffn_w_gu[i // 2], ffn_w_down[i // 2])
            if not last:
                xc = _dense_block(xc, (csc2, csh2), cg2, *ln2, ffn_w_gu[i // 2], ffn_w_down[i // 2])
        else:
            moe = (moe_router[i // 2], moe_w_gu[i // 2], moe_w_down[i // 2])
            x = _layer_norm(DEEPNORM_ALPHA * x + g2 * _moe(x * (1 + sc2) + sh2, *moe), *ln2)
            if not last:
                xc = _layer_norm(DEEPNORM_ALPHA * xc + cg2 * _moe(xc * (1 + csc2) + csh2, *moe), *ln2)
    return x
```

```python
import functools
import math

import jax
import jax.numpy as jnp
from jax import lax
from jax.experimental import pallas as pl
from jax.experimental.pallas import tpu as pltpu

F32 = jnp.float32
BF16 = jnp.bfloat16
HIGHEST = lax.Precision.HIGHEST

D_MODEL = 1024
DEPTH = 4
GRID_W = 64
LN_EPS = 1e-5
NORM_EPS = 1e-6
DEEPNORM_ALPHA = (2.0 * DEPTH) ** 0.25
CHUNK = 64
N_MIXERS = 3
LANES = 128

GLA_HEADS = 4
GLA_DK = D_MODEL // (2 * GLA_HEADS)
GLA_DV = D_MODEL // GLA_HEADS
GLA_KEY = GLA_HEADS * GLA_DK
GLA_VAL = GLA_HEADS * GLA_DV
GLA_RANK = 16
GLA_TAU = 16.0

DN_HEADS = 8
DN_DK = D_MODEL // DN_HEADS
DN_DV = D_MODEL // DN_HEADS
DN_KEY = DN_HEADS * DN_DK
DN_VAL = DN_HEADS * DN_DV

HY_ORDER = 2
HY_SIDES = 2
HY_BANDS = 16
HY_SHIFT = 0.05

N_EXPERTS = 8
TOP_K = 2
MOE_TILE = 512

VMEM_LIMIT_BYTES = 56 * 1024 * 1024

_NT = (((1,), (1,)), ((), ()))
_TN = (((0,), (0,)), ((), ()))


def _cparams(sem):
    return pltpu.CompilerParams(dimension_semantics=sem, vmem_limit_bytes=VMEM_LIMIT_BYTES)


def _ln_rows(y, g, b):
    mu = jnp.mean(y, axis=-1, keepdims=True)
    yc = y - mu
    var = jnp.mean(yc * yc, axis=-1, keepdims=True)
    return yc * lax.rsqrt(var + LN_EPS) * g + b


def _mm_kernel(*refs, nk, mod, ln):
    refs = list(refs)
    a_ref, b_ref = refs[:2]
    o_ref = refs[-1]
    extra = refs[2:-1]
    a = a_ref[...]
    if mod:
        a = a * (1.0 + extra[0][0]) + extra[1][0]
        extra = extra[2:]
    part = jnp.dot(a.astype(BF16), b_ref[...], preferred_element_type=F32)
    if ln:
        res_ref, gate_ref, g_ref, bias_ref = extra
        o_ref[...] = _ln_rows(DEEPNORM_ALPHA * res_ref[...] + gate_ref[0] * part, g_ref[...], bias_ref[...])
    elif nk == 1:
        o_ref[...] = part
    else:
        k = pl.program_id(2)

        @pl.when(k == 0)
        def _():
            o_ref[...] = part

        @pl.when(k > 0)
        def _():
            o_ref[...] += part


def _pick(n, cands):
    for c in cands:
        if n % c == 0:
            return c
    return n


def _lane_tile(n, cap):
    best = LANES
    for t in range(LANES, cap + 1, LANES):
        if n % t == 0:
            best = t
    return best


def _mm(a, w, mod=None, ln=None):
    M, K = a.shape
    N = w.shape[1]
    n_pad = (-N) % LANES
    wb = w.astype(BF16)
    if n_pad:
        wb = jnp.pad(wb, ((0, 0), (0, n_pad)))
    Np = N + n_pad
    groups = mod[0].shape[0] if mod is not None else (ln[1].shape[0] if ln is not None else 1)
    rows = M // groups
    tm = _pick(rows, (1024, 512, 256, 128, 64, 32, 16, 8))
    tn = Np if ln is not None else _lane_tile(Np, 1408)
    tk = K if (mod is not None or ln is not None) else _pick(K, (1024, 512, 256, 128))
    nk = K // tk
    per_group = lambda width: pl.BlockSpec((1, 1, width), lambda i, j, k: ((i * tm) // rows, 0, 0))
    specs = [pl.BlockSpec((tm, tk), lambda i, j, k: (i, k)),
             pl.BlockSpec((tk, tn), lambda i, j, k: (k, j))]
    args = [a, wb]
    if mod is not None:
        specs += [per_group(K), per_group(K)]
        args += list(mod)
    if ln is not None:
        assert n_pad == 0
        row_vec = pl.BlockSpec((1, N), lambda i, j, k: (0, 0))
        specs += [pl.BlockSpec((tm, N), lambda i, j, k: (i, 0)), per_group(N), row_vec, row_vec]
        args += list(ln)
    return pl.pallas_call(
        functools.partial(_mm_kernel, nk=nk, mod=mod is not None, ln=ln is not None),
        grid=(M // tm, Np // tn, nk),
        in_specs=specs,
        out_specs=pl.BlockSpec((tm, tn), lambda i, j, k: (i, j)),
        out_shape=jax.ShapeDtypeStruct((M, Np), F32),
        compiler_params=_cparams(("parallel", "parallel", "arbitrary")),
        name="mm",
    )(*args)


def _mm3(u, w, mod=None):
    B, L, K = u.shape
    return _mm(u.reshape(B * L, K), w, mod=mod).reshape(B, L, -1)


def _ffn_kernel(*refs, nf, block, row_scale):
    refs = list(refs)
    u_ref, wg_ref, wu_ref, wd_ref = refs[1:5]
    o_ref, acc_ref = refs[-2:]
    extra = refs[5:-2]
    f = pl.program_id(1)

    @pl.when(f == 0)
    def _():
        acc_ref[...] = jnp.zeros_like(acc_ref)

    u = u_ref[...]
    if block:
        u = u * (1.0 + extra[0][0]) + extra[1][0]
    u = u.astype(BF16)
    g = jnp.dot(u, wg_ref[0], preferred_element_type=F32)
    up = jnp.dot(u, wu_ref[0], preferred_element_type=F32)
    act = (g * jax.nn.sigmoid(g) * up).astype(BF16)
    acc_ref[...] += jnp.dot(act, wd_ref[0], preferred_element_type=F32)

    @pl.when(f == nf - 1)
    def _():
        y = acc_ref[...]
        if row_scale:
            y = y * extra[0][...]
        if block:
            y = _ln_rows(DEEPNORM_ALPHA * u_ref[...] + extra[2][0] * y, extra[3][...], extra[4][...])
        o_ref[...] = y.astype(o_ref.dtype)


def _ffn(u, w_gu, w_down, tile_expert, tm, block=None, row_scale=None, out_dtype=F32):
    M, D = u.shape
    F = w_down.shape[1]
    tf = _pick(F, (1792, 1408, 512, 256, 128))
    nf = F // tf
    specs = [pl.BlockSpec((tm, D), lambda i, f, te: (i, 0)),
             pl.BlockSpec((1, D, tf), lambda i, f, te: (te[i], 0, f)),
             pl.BlockSpec((1, D, tf), lambda i, f, te: (te[i], 0, f + nf)),
             pl.BlockSpec((1, tf, D), lambda i, f, te: (te[i], f, 0))]
    args = [tile_expert, u, w_gu, w_gu, w_down]
    if block is not None:
        rows = M // block[0].shape[0]
        assert rows % tm == 0
        per_group = pl.BlockSpec((1, 1, D), lambda i, f, te: ((i * tm) // rows, 0, 0))
        row_vec = pl.BlockSpec((1, D), lambda i, f, te: (0, 0))
        specs += [per_group, per_group, per_group, row_vec, row_vec]
        args += list(block)
    if row_scale is not None:
        specs.append(pl.BlockSpec((tm, 1), lambda i, f, te: (i, 0)))
        args.append(row_scale)
    grid_spec = pltpu.PrefetchScalarGridSpec(
        num_scalar_prefetch=1,
        grid=(M // tm, nf),
        in_specs=specs,
        out_specs=pl.BlockSpec((tm, D), lambda i, f, te: (i, 0)),
        scratch_shapes=[pltpu.VMEM((tm, D), F32)])
    return pl.pallas_call(
        functools.partial(_ffn_kernel, nf=nf, block=block is not None, row_scale=row_scale is not None),
        grid_spec=grid_spec,
        out_shape=jax.ShapeDtypeStruct((M, D), out_dtype),
        compiler_params=_cparams(("parallel", "arbitrary")),
        name="ffn",
    )(*args)


def _grid_sincos(rows, dtype):
    r, col = jnp.meshgrid(jnp.arange(rows, dtype=F32), jnp.arange(GRID_W, dtype=F32), indexing='ij')
    n_freq = D_MODEL // 4
    omega = 1.0 / (10000.0 ** (jnp.arange(n_freq, dtype=F32) / n_freq))

    def emb(p):
        ang = p.reshape(-1)[:, None] * omega[None, :]
        return jnp.concatenate([jnp.sin(ang), jnp.cos(ang)], -1)

    return jnp.concatenate([emb(r), emb(col)], -1).astype(dtype)


def _bdot(a, b, dims=None):
    a = a.astype(BF16)
    b = b.astype(BF16)
    if dims is None:
        return jnp.dot(a, b, preferred_element_type=F32)
    return lax.dot_general(a, b, dims, preferred_element_type=F32)


GLA_CHUNK = 128
GLA_BPS = 2


def _scan_specs(nc, rev, widths_and_blocks):
    def cidx(j):
        return nc - 1 - j if rev else j
    return [pl.BlockSpec((GLA_BPS, GLA_CHUNK, w), functools.partial(lambda b, j, cb: (b, cidx(j), cb), cb=cb))
            for w, cb in widths_and_blocks]


def _head_readout(o_h, norm_g, gate):
    inv = lax.rsqrt(jnp.mean(o_h * o_h, axis=-1, keepdims=True) + NORM_EPS)
    return o_h * inv * norm_g * (gate * jax.nn.sigmoid(gate))


def _gla_kernel(*refs, rev, nc, with_prev, readout):
    refs = list(refs)
    q_ref, k_ref, v_ref, t_ref, w2_ref, gb_ref, s0_ref = refs[:7]
    o_ref, sfin_ref, s_scr = refs[-3:]
    extra = refs[7:-3]
    prev_ref = extra.pop(0) if with_prev else None
    gate_ref, ng_ref = extra if readout else (None, None)
    j = pl.program_id(1)

    @pl.when(j == 0)
    def _():
        s_scr[...] = s0_ref[...]

    row = lax.broadcasted_iota(jnp.int32, (GLA_CHUNK, GLA_CHUNK), 0)
    col = lax.broadcasted_iota(jnp.int32, (GLA_CHUNK, GLA_CHUNK), 1)
    incl = (col >= row) if rev else (col <= row)
    last = 0 if rev else GLA_CHUNK - 1
    mid = GLA_CHUNK // 2
    scale = GLA_DK ** -0.5
    chains = [(bi, h) for bi in range(GLA_BPS) for h in range(GLA_HEADS)]
    ks = {h: slice(h * GLA_DK, (h + 1) * GLA_DK) for h in range(GLA_HEADS)}
    vs = {h: slice(h * GLA_DV, (h + 1) * GLA_DV) for h in range(GLA_HEADS)}
    b_all = []
    for bi in range(GLA_BPS):
        z = _bdot(t_ref[bi], w2_ref[...]) + gb_ref[...]
        log_a = (jnp.minimum(z, 0.0) - jnp.log(1.0 + jnp.exp(-jnp.abs(z)))) * (1.0 / GLA_TAU)
        b_all.append(jnp.dot(incl.astype(F32), log_a, precision=HIGHEST, preferred_element_type=F32))
    bhs = [b_all[bi][:, ks[h]] for bi, h in chains]
    b_lasts = [bh[last:last + 1, :] for bh in bhs]
    b_mids = [bh[mid:mid + 1, :] for bh in bhs]
    qfs = [q_ref[bi, :, ks[h]] * scale for bi, h in chains]
    kfs = [k_ref[bi, :, ks[h]] for bi, h in chains]
    vbs = [v_ref[bi, :, vs[h]].astype(BF16) for bi, h in chains]
    sts = [s_scr[bi, h] for bi, h in chains]
    n = range(len(chains))
    q_piv = [(qfs[i] * jnp.exp(bhs[i] - b_mids[i])).astype(BF16) for i in n]
    k_piv = [(kfs[i] * jnp.exp(b_mids[i] - bhs[i])).astype(BF16) for i in n]
    scores = [jnp.where(incl, _bdot(q_piv[i], k_piv[i], _NT), 0.0) for i in n]
    inter = [_bdot(qfs[i] * jnp.exp(bhs[i]), sts[i], _NT) for i in n]
    intra = [_bdot(scores[i], vbs[i]) for i in n]
    kv = [_bdot(vbs[i], kfs[i] * jnp.exp(b_lasts[i] - bhs[i]), _TN) for i in n]
    for i, (bi, h) in enumerate(chains):
        s_scr[bi, h] = sts[i] * jnp.exp(b_lasts[i]) + kv[i]
        o_h = intra[i] + inter[i]
        if prev_ref is not None:
            o_h = o_h + prev_ref[bi, :, vs[h]]
        if readout:
            o_h = _head_readout(o_h, ng_ref[...], gate_ref[bi, :, vs[h]])
        o_ref[bi, :, vs[h]] = o_h.astype(o_ref.dtype)

    @pl.when(j == nc - 1)
    def _():
        sfin_ref[...] = s_scr[...]


def _gla_scan(proj, w2, gb, s0, prev, rev, norm_g=None):
    B, L, _ = proj.shape
    assert L % GLA_CHUNK == 0 and B % GLA_BPS == 0
    nc = L // GLA_CHUNK
    with_prev = prev is not None
    readout = norm_g is not None
    gate_cb = (2 * GLA_KEY + 2 * GLA_VAL) // LANES
    state = pl.BlockSpec((GLA_BPS, GLA_HEADS, GLA_DV, GLA_DK), lambda b, j: (b, 0, 0, 0))
    specs = _scan_specs(nc, rev, [(GLA_KEY, 0), (GLA_KEY, 1), (GLA_VAL, 1), (LANES, gate_cb)])
    specs += [pl.BlockSpec((LANES, GLA_KEY), lambda b, j: (0, 0)),
              pl.BlockSpec((1, GLA_KEY), lambda b, j: (0, 0)),
              state]
    args = [proj, proj, proj, proj, w2, gb, s0]
    o_spec = _scan_specs(nc, rev, [(GLA_VAL, 0)])[0]
    if with_prev:
        specs.append(o_spec)
        args.append(prev)
    if readout:
        specs += [_scan_specs(nc, rev, [(GLA_VAL, 2)])[0], pl.BlockSpec((1, GLA_DV), lambda b, j: (0, 0))]
        args += [proj, norm_g.reshape(1, GLA_DV)]
    return pl.pallas_call(
        functools.partial(_gla_kernel, rev=rev, nc=nc, with_prev=with_prev, readout=readout),
        grid=(B // GLA_BPS, nc),
        in_specs=specs,
        out_specs=[o_spec, state],
        out_shape=[jax.ShapeDtypeStruct((B, L, GLA_VAL), BF16 if readout else F32),
                   jax.ShapeDtypeStruct((B, GLA_HEADS, GLA_DV, GLA_DK), F32)],
        scratch_shapes=[pltpu.VMEM((GLA_BPS, GLA_HEADS, GLA_DV, GLA_DK), F32)],
        compiler_params=_cparams(("parallel", "arbitrary")),
        name="gla_scan_rev" if rev else "gla_scan_fwd",
    )(*args)


def _gla_mixer(x_c, mod_c, x_x, mod_x, w_in, gate_w1, gate_w2, gate_b, norm_g):
    w_ext = jnp.concatenate([w_in, gate_w1[0], gate_w1[1]], axis=1)
    w2 = []
    for d in range(2):
        w2.append(jnp.zeros((LANES, GLA_KEY), F32).at[d * GLA_RANK:(d + 1) * GLA_RANK].set(gate_w2[d]).astype(BF16))
    gb = gate_b[:, None, :]

    def bidir(proj, s0f, s0b):
        o, s_f = _gla_scan(proj, w2[0], gb[0], s0f, None, False)
        y, s_b = _gla_scan(proj, w2[1], gb[1], s0b, o, True, norm_g=norm_g)
        return y, s_f, s_b

    s0 = jnp.zeros((x_c.shape[0], GLA_HEADS, GLA_DV, GLA_DK), F32)
    y_c, s_f, s_b = bidir(_mm3(x_c, w_ext, mod=mod_c), s0, s0)
    y_x, _, _ = bidir(_mm3(x_x, w_ext, mod=mod_x), s_f, s_b)
    return y_c, y_x


def _dn_prep_kernel(x_ref, w_ref, o_ref, *, L):
    cb = pl.program_id(1)
    y = _dwconv3_rows(x_ref[0], w_ref[...])
    y = y * jax.nn.sigmoid(y)
    inv = lax.rsqrt(jnp.sum(y * y, axis=1, keepdims=True) + NORM_EPS)
    q_scale = jnp.where(cb < DN_HEADS, DN_DK ** -0.5, 1.0)
    fac = jnp.where(cb < 2 * DN_HEADS, inv * q_scale, 1.0)
    o_ref[0] = (y * fac).astype(BF16)


def _dn_prep(proj, conv_w):
    B, L, _ = proj.shape
    ncb = (2 * DN_KEY + DN_VAL) // LANES
    return pl.pallas_call(
        functools.partial(_dn_prep_kernel, L=L),
        grid=(B, ncb),
        in_specs=[pl.BlockSpec((1, L, LANES), lambda b, c: (b, 0, c)),
                  pl.BlockSpec((conv_w.shape[0], LANES), lambda b, c: (0, c))],
        out_specs=pl.BlockSpec((1, L, LANES), lambda b, c: (b, 0, c)),
        out_shape=jax.ShapeDtypeStruct((B, L, ncb * LANES), BF16),
        compiler_params=_cparams(("parallel", "parallel")),
        name="dn_prep",
    )(proj, conv_w)


DN_GROUP = 4
DN_ROWS = DN_GROUP * CHUNK


DN_CPS = 2
DN_BPS = 2
DN_LEVELS = CHUNK.bit_length() - 1
M_INCL, M_STRICT, M_EYE, M_LEVEL0 = 0, 1, 2, 3


def _dn_masks(rev):
    r = jnp.arange(DN_ROWS, dtype=jnp.int32)[:, None]
    c = jnp.arange(DN_ROWS, dtype=jnp.int32)[None, :]
    same_head = (r // CHUNK) == (c // CHUNK)
    rt, ct = r % CHUNK, c % CHUNK
    out = [same_head & ((ct >= rt) if rev else (ct <= rt)), same_head & ((ct > rt) if rev else (ct < rt)), r == c]
    for lv in range(DN_LEVELS):
        s = 1 << lv
        r_hi, c_hi = (r // s) % 2 == 1, (c // s) % 2 == 1
        out.append(((r // (2 * s)) == (c // (2 * s))) & ((c_hi & ~r_hi) if rev else (r_hi & ~c_hi)))
    return jnp.stack(out).astype(F32)


def _dn_chunk_kernel(q_ref, k_ref, v_ref, ba_ref, par_ref, m_ref, p_ref, qm_ref, g_ref, r_ref, oi_ref, *, rev):
    prow = lax.broadcasted_iota(jnp.int32, (2 * CHUNK, CHUNK), 0)
    pcol = lax.broadcasted_iota(jnp.int32, (2 * CHUNK, CHUNK), 1)
    tri_pad = (((pcol >= prow) if rev else (pcol <= prow)) & (prow < CHUNK)).astype(F32)
    d = 1 if rev else 0
    last = 0 if rev else CHUNK - 1
    incl = m_ref[M_INCL]
    eye = m_ref[M_EYE]
    items = []
    for ci in range(DN_CPS):
        tr = slice(ci * CHUNK, (ci + 1) * CHUNK)
        ba = ba_ref[0, tr, :]
        beta_all = jax.nn.sigmoid(ba)
        sp_in = ba + par_ref[1:2, :]
        softplus = jnp.maximum(sp_in, 0.0) + jnp.log(1.0 + jnp.exp(-jnp.abs(sp_in)))
        g_all = par_ref[0:1, :] * softplus
        gc_pad = jnp.dot(tri_pad, g_all, precision=HIGHEST, preferred_element_type=F32)
        gc = gc_pad[:CHUNK]
        gc_t = gc_pad.T
        for grp in range(DN_HEADS // DN_GROUP):
            heads = range(grp * DN_GROUP, (grp + 1) * DN_GROUP)
            gls = [2 * DN_HEADS + d * DN_HEADS + h for h in heads]
            qs = jnp.concatenate([q_ref[0, tr, h * DN_DK:(h + 1) * DN_DK] for h in heads], axis=0)
            ks = jnp.concatenate([k_ref[0, tr, h * DN_DK:(h + 1) * DN_DK] for h in heads], axis=0)
            vs = jnp.concatenate([v_ref[0, tr, h * DN_DV:(h + 1) * DN_DV] for h in heads], axis=0)
            beta = jnp.concatenate([beta_all[:, d * DN_HEADS + h:d * DN_HEADS + h + 1] for h in heads], axis=0)
            gcol = jnp.concatenate([gc[:, gl:gl + 1] for gl in gls], axis=0)
            glast = jnp.concatenate([jnp.broadcast_to(gc[last:last + 1, gl:gl + 1], (CHUNK, 1)) for gl in gls],
                                    axis=0)
            rows = [gc_t[gl:gl + 1, :] for gl in gls]
            grow = jnp.concatenate([rows[0] + pltpu.roll(rows[1], CHUNK, 1),
                                    rows[2] + pltpu.roll(rows[3], CHUNK, 1)], axis=1)
            items.append(dict(ci=ci, tr=tr, heads=heads, qs=qs, ks=ks, vs=vs, beta=beta, gcol=gcol, glast=glast,
                              grow=grow, gfin=[gc[last:last + 1, gl:gl + 1] for gl in gls]))
    dms = [jnp.exp(jnp.minimum(it["gcol"] - it["grow"], 0.0)) * incl for it in items]
    kfs = [it["ks"].astype(F32) for it in items]
    kbetas = [kf * it["beta"] for kf, it in zip(kfs, items)]
    a_s = [_bdot(kb, it["ks"], _NT) * (dm * m_ref[M_STRICT]) for kb, it, dm in zip(kbetas, items, dms)]
    ts = [eye - a * m_ref[M_LEVEL0] for a in a_s]
    for lv in range(1, DN_LEVELS):
        mts = [_bdot(a * m_ref[M_LEVEL0 + lv], t) for a, t in zip(a_s, ts)]
        ts = [t - _bdot(t, mt) for t, mt in zip(ts, mts)]
    rhs = [jnp.concatenate([it["vs"].astype(F32) * it["beta"], kb * jnp.exp(it["gcol"])], axis=1)
           for it, kb in zip(items, kbetas)]
    sols = [_bdot(t, r) for t, r in zip(ts, rhs)]
    qks = [_bdot(it["qs"], it["ks"], _NT) * dm for it, dm in zip(items, dms)]
    qsols = [_bdot(qk, sol) for qk, sol in zip(qks, sols)]
    for it, kf, sol, qsol in zip(items, kfs, sols, qsols):
        ci, tr = it["ci"], it["tr"]
        k_dec = kf * jnp.exp(it["glast"] - it["gcol"])
        r_all = it["qs"].astype(F32) * jnp.exp(it["gcol"]) - qsol[:, DN_DV:]
        for i, h in enumerate(it["heads"]):
            rs = slice(i * CHUNK, (i + 1) * CHUNK)
            hs = slice(h * DN_DV, (h + 1) * DN_DV)
            pq = _bdot(k_dec[rs], sol[rs], _TN)
            qm_ref[0, ci, h] = pq[:, :DN_DV]
            p_ref[0, ci, h] = (-pq[:, DN_DV:]).astype(BF16)
            g_ref[0, ci, h:h + 1, :] = jnp.broadcast_to(jnp.exp(it["gfin"][i]), (1, DN_DV))
            r_ref[0, tr, hs] = r_all[rs].astype(BF16)
            oi_ref[0, tr, hs] = qsol[rs, :DN_DV]


def _dn_chunk(qkv, proj, par, rev):
    B, L, _ = qkv.shape
    nc = L // CHUNK
    assert nc % DN_CPS == 0
    ba_cb = (2 * DN_KEY + 2 * DN_VAL) // LANES
    tok = lambda w, cb: pl.BlockSpec((1, DN_CPS * CHUNK, w), lambda b, j: (b, j, cb))
    mat = pl.BlockSpec((1, DN_CPS, DN_HEADS, DN_DK, DN_DV), lambda b, j: (b, j, 0, 0, 0))
    masks = _dn_masks(rev)
    return pl.pallas_call(
        functools.partial(_dn_chunk_kernel, rev=rev),
        grid=(B, nc // DN_CPS),
        in_specs=[tok(DN_KEY, 0), tok(DN_KEY, 1), tok(DN_VAL, 2), tok(LANES, ba_cb),
                  pl.BlockSpec((2, LANES), lambda b, j: (0, 0)),
                  pl.BlockSpec(masks.shape, lambda b, j: (0, 0, 0), pipeline_mode=pl.Buffered(1))],
        out_specs=[mat, mat, pl.BlockSpec((1, DN_CPS, DN_HEADS, DN_DV), lambda b, j: (b, j, 0, 0)),
                   tok(DN_KEY, 0), tok(DN_VAL, 0)],
        out_shape=[jax.ShapeDtypeStruct((B, nc, DN_HEADS, DN_DK, DN_DV), BF16),
                   jax.ShapeDtypeStruct((B, nc, DN_HEADS, DN_DK, DN_DV), F32),
                   jax.ShapeDtypeStruct((B, nc, DN_HEADS, DN_DV), F32),
                   jax.ShapeDtypeStruct((B, L, DN_KEY), BF16),
                   jax.ShapeDtypeStruct((B, L, DN_VAL), F32)],
        compiler_params=_cparams(("parallel", "parallel")),
        name="dn_chunk_rev" if rev else "dn_chunk_fwd",
    )(qkv, qkv, qkv, proj, par, masks)


def _dn_seq_kernel(*refs, nc, with_prev, readout):
    refs = list(refs)
    p_ref, qm_ref, g_ref, r_ref, oi_ref, s0_ref = refs[:6]
    o_ref, sfin_ref, s_scr = refs[-3:]
    extra = refs[6:-3]
    prev_ref = extra.pop(0) if with_prev else None
    gate_ref, ng_ref = extra if readout else (None, None)
    j = pl.program_id(1)

    @pl.when(j == 0)
    def _():
        s_scr[...] = s0_ref[...]

    chains = [(bi, h) for bi in range(DN_BPS) for h in range(DN_HEADS)]
    states = [s_scr[bi, h] for bi, h in chains]
    xs = []
    for i, (bi, h) in enumerate(chains):
        lhs = jnp.concatenate([p_ref[bi, 0, h], r_ref[bi, :, h * DN_DK:(h + 1) * DN_DK]], axis=0)
        xs.append(_bdot(lhs, states[i]))
    for i, (bi, h) in enumerate(chains):
        hs = slice(h * DN_DV, (h + 1) * DN_DV)
        s_scr[bi, h] = g_ref[bi, 0, h:h + 1, :] * states[i] + xs[i][:DN_DK] + qm_ref[bi, 0, h]
        o_h = xs[i][DN_DK:] + oi_ref[bi, :, hs]
        if prev_ref is not None:
            o_h = o_h + prev_ref[bi, :, hs]
        if readout:
            o_h = _head_readout(o_h, ng_ref[...], gate_ref[bi, :, hs])
        o_ref[bi, :, hs] = o_h.astype(o_ref.dtype)

    @pl.when(j == nc - 1)
    def _():
        sfin_ref[...] = s_scr[...]


def _dn_seq(chunk_terms, s0, prev, rev, readout=None):
    p, qm, g, r, oi = chunk_terms
    B, L, _ = r.shape
    nc = L // CHUNK
    with_prev = prev is not None

    def cidx(j):
        return nc - 1 - j if rev else j

    assert B % DN_BPS == 0
    mat = pl.BlockSpec((DN_BPS, 1, DN_HEADS, DN_DK, DN_DV), lambda b, j: (b, cidx(j), 0, 0, 0))
    tok = pl.BlockSpec((DN_BPS, CHUNK, DN_VAL), lambda b, j: (b, cidx(j), 0))
    state = pl.BlockSpec((DN_BPS, DN_HEADS, DN_DK, DN_DV), lambda b, j: (b, 0, 0, 0))
    specs = [mat, mat, pl.BlockSpec((DN_BPS, 1, DN_HEADS, DN_DV), lambda b, j: (b, cidx(j), 0, 0)), tok, tok, state]
    args = [p, qm, g, r, oi, s0]
    if with_prev:
        specs.append(tok)
        args.append(prev)
    if readout is not None:
        z_cb = (2 * DN_KEY + DN_VAL) // DN_VAL
        specs += [pl.BlockSpec((DN_BPS, CHUNK, DN_VAL), lambda b, j: (b, cidx(j), z_cb)),
                  pl.BlockSpec((1, DN_DV), lambda b, j: (0, 0))]
        args += [readout[0], readout[1].reshape(1, DN_DV)]
    return pl.pallas_call(
        functools.partial(_dn_seq_kernel, nc=nc, with_prev=with_prev, readout=readout is not None),
        grid=(B // DN_BPS, nc),
        in_specs=specs,
        out_specs=[tok, state],
        out_shape=[jax.ShapeDtypeStruct((B, L, DN_VAL), BF16 if readout is not None else F32),
                   jax.ShapeDtypeStruct((B, DN_HEADS, DN_DK, DN_DV), F32)],
        scratch_shapes=[pltpu.VMEM((DN_BPS, DN_HEADS, DN_DK, DN_DV), F32)],
        compiler_params=_cparams(("parallel", "arbitrary")),
        name="dn_seq_rev" if rev else "dn_seq_fwd",
    )(*args)


def _delta_mixer(x_c, mod_c, x_x, mod_x, w_in, conv_w, A_log, dt_bias, norm_g):
    a_lanes = slice(2 * DN_HEADS, 4 * DN_HEADS)
    par = jnp.zeros((2, LANES), F32)
    par = par.at[0, a_lanes].set(-jnp.exp(A_log).reshape(-1)).at[1, a_lanes].set(dt_bias.reshape(-1))

    def bidir(proj, s0f, s0b):
        qkv = _dn_prep(proj, conv_w)
        o, s_f = _dn_seq(_dn_chunk(qkv, proj, par, False), s0f, None, False)
        y, s_b = _dn_seq(_dn_chunk(qkv, proj, par, True), s0b, o, True, readout=(proj, norm_g))
        return y, s_f, s_b

    s0 = jnp.zeros((x_c.shape[0], DN_HEADS, DN_DK, DN_DV), F32)
    y_c, s_f, s_b = bidir(_mm3(x_c, w_in, mod=mod_c), s0, s0)
    y_x, _, _ = bidir(_mm3(x_x, w_in, mod=mod_x), s_f, s_b)
    return y_c, y_x


def _hyena_filter_spectrum(L, pe_w1, pe_b1, freq, pe_w2, pe_b2, pe_w3, decay):
    t = jnp.arange(L, dtype=F32)
    tn = t / L
    ang = (2.0 * math.pi / L) * t[:, None] * jnp.arange(1, HY_BANDS + 1, dtype=F32)[None, :]
    feats = jnp.concatenate([tn[:, None], jnp.cos(ang), jnp.sin(ang)], -1)
    h = jnp.sin(freq * (feats @ pe_w1 + pe_b1))
    h = jnp.sin(freq * (h @ pe_w2 + pe_b2))
    h = (h @ pe_w3) * (jnp.exp(-tn[:, None] * decay) + HY_SHIFT)
    h = h.reshape(L, HY_ORDER, HY_SIDES, D_MODEL)
    h_circ = jnp.concatenate([h[:, :, 0], jnp.zeros((1, HY_ORDER, D_MODEL), F32), h[:0:-1, :, 1]], axis=0)
    return jnp.fft.rfft(h_circ, axis=0)


HY_TD = 256
HY_TK = 512


def _dwconv3_rows(x, w):
    L = x.shape[0]
    rowi = lax.broadcasted_iota(jnp.int32, x.shape, 0)
    x_prev = jnp.where(rowi == 0, 0.0, pltpu.roll(x, 1, 0))
    x_next = jnp.where(rowi == L - 1, 0.0, pltpu.roll(x, L - 1, 0))
    return w[0:1] * x_prev + w[1:2] * x + w[2:3] * x_next


def _hy_tables(L):
    n = 2 * L
    k = jnp.arange(L, dtype=jnp.int32)
    ang = (2.0 * math.pi / n) * ((k[:, None] * k[None, :]) % n).astype(F32)
    sign = 1.0 - 2.0 * (k % 2).astype(F32)
    ws = (-jnp.sin(ang)).at[0, :].set(sign)
    return jnp.cos(ang).astype(BF16), ws.astype(BF16), ws.T.astype(BF16)


def _hy_filter(L, filt):
    spec = _hyena_filter_spectrum(L, *filt)
    n = 2 * L
    scale = jnp.full((L, 1, 1), 2.0 / n, F32).at[0].set(1.0 / n)
    hr = jnp.real(spec[:L]) * scale
    hi = (jnp.imag(spec[:L]) * scale).at[0].set(0.0)
    hr2 = hr.at[0].set(jnp.real(spec[L]) / n)
    return tuple(jnp.transpose(a, (1, 0, 2)) for a in (hr, hi, hr2))


def _hy_fwd_kernel(x_ref, cw_ref, c_ref, ws_ref, hr_ref, hi_ref, hr2_ref, yr_ref, yi_ref, *, L, conv):
    x = x_ref[0]
    vb = (_dwconv3_rows(x, cw_ref[...]) if conv else x).astype(BF16)
    tk = min(HY_TK, L)
    for kb in range(L // tk):
        rows = slice(kb * tk, (kb + 1) * tk)
        ur = jnp.dot(c_ref[rows, :], vb, preferred_element_type=F32)
        ui = jnp.dot(ws_ref[rows, :], vb, preferred_element_type=F32)
        hr = hr_ref[rows, :]
        hi = hi_ref[rows, :]
        yr_ref[0, rows, :] = (ur * hr - ui * hi).astype(BF16)
        yi_ref[0, rows, :] = (ur * hi + ui * hr2_ref[rows, :]).astype(BF16)


def _hy_inv_kernel(yr_ref, yi_ref, c_ref, wst_ref, g_ref, gw_ref, v_ref, vw_ref, sk_ref, o_ref, *, L, conv_v):
    gate = _dwconv3_rows(g_ref[0], gw_ref[...])
    v = v_ref[0]
    if conv_v:
        v = _dwconv3_rows(v, vw_ref[...])
    yr = yr_ref[0]
    yi = yi_ref[0]
    tk = min(HY_TK, L)
    for tb in range(L // tk):
        rows = slice(tb * tk, (tb + 1) * tk)
        y = (jnp.dot(c_ref[rows, :], yr, preferred_element_type=F32)
             + jnp.dot(wst_ref[rows, :], yi, preferred_element_type=F32))
        o_ref[0, rows, :] = (gate[rows, :] * (y + sk_ref[...] * v[rows, :])).astype(o_ref.dtype)


def _hy_specs(L, td):
    table = pl.BlockSpec((L, L), lambda d, b: (0, 0), pipeline_mode=pl.Buffered(1))
    col = lambda off: pl.BlockSpec((1, L, td), functools.partial(lambda d, b, off: (b, 0, off + d), off=off))
    par = lambda r: pl.BlockSpec((r, td), lambda d, b: (0, d))
    return table, col, par


def _hy_fwd(x, col_off, conv_w, c, ws, hr, hi, hr2):
    B, L, _ = x.shape
    D = hr.shape[1]
    td = HY_TD
    table, col, par = _hy_specs(L, td)
    conv = conv_w is not None
    if not conv:
        conv_w = jnp.zeros((3, D), F32)
    out = jax.ShapeDtypeStruct((B, L, D), BF16)
    return pl.pallas_call(
        functools.partial(_hy_fwd_kernel, L=L, conv=conv),
        grid=(D // td, B),
        in_specs=[col(col_off), par(3), table, table, par(L), par(L), par(L)],
        out_specs=[col(0), col(0)],
        out_shape=[out, out],
        compiler_params=_cparams(("parallel", "parallel")),
        name="hy_fwd",
    )(x, conv_w, c, ws, hr, hi, hr2)


def _hy_inv(yr, yi, c, wst, proj, gate_off, gate_w, sig, sig_off, sig_w, skip, out_dtype=F32):
    B, L, D = yr.shape
    td = HY_TD
    table, col, par = _hy_specs(L, td)
    conv_v = sig_w is not None
    if not conv_v:
        sig_w = jnp.zeros((3, D), F32)
    return pl.pallas_call(
        functools.partial(_hy_inv_kernel, L=L, conv_v=conv_v),
        grid=(D // td, B),
        in_specs=[col(0), col(0), table, table, col(gate_off), par(3), col(sig_off), par(3), par(1)],
        out_specs=col(0),
        out_shape=jax.ShapeDtypeStruct((B, L, D), out_dtype),
        compiler_params=_cparams(("parallel", "parallel")),
        name="hy_inv",
    )(yr, yi, c, wst, proj, gate_w, sig, sig_w, skip)


def _hyena_seq(x, mod, w_in, conv_w, filt, skip):
    B, L, D = x.shape
    assert conv_w.shape[0] == 3 and D % HY_TD == 0
    nd = D // HY_TD
    proj = _mm3(x, w_in, mod=mod)
    cw = [conv_w[:, i * D:(i + 1) * D] for i in range(3)]
    c, ws, wst = _hy_tables(L)
    hr, hi, hr2 = _hy_filter(L, filt)
    yr, yi = _hy_fwd(proj, 2 * nd, cw[2], c, ws, hr[0], hi[0], hr2[0])
    z = _hy_inv(yr, yi, c, wst, proj, 0, cw[0], proj, 2 * nd, cw[2], skip[0:1])
    yr, yi = _hy_fwd(z, 0, None, c, ws, hr[1], hi[1], hr2[1])
    return _hy_inv(yr, yi, c, wst, proj, nd, cw[1], z, 0, None, skip[1:2], out_dtype=BF16)


def _dense_block(x, mod, gate, ln_g, ln_b, w_gu, w_down):
    B, L, D = x.shape
    M = B * L
    tm = _pick(M // mod[0].shape[0], (512, 256, 128))
    te = jnp.zeros((M // tm,), jnp.int32)
    block = (mod[0], mod[1], gate, ln_g, ln_b)
    return _ffn(x.reshape(M, D), w_gu.astype(BF16)[None], w_down.astype(BF16)[None], te, tm,
                block=block).reshape(B, L, D)


def _route_kernel(x_ref, sc_ref, sh_ref, r_ref, h_ref, lg_ref):
    h = x_ref[...] * (1.0 + sc_ref[0]) + sh_ref[0]
    h_ref[...] = h.astype(BF16)
    lg_ref[...] = jnp.dot(h, r_ref[...], precision=HIGHEST, preferred_element_type=F32)


def _route(x, mod, router):
    M, D = x.shape
    rows = M // mod[0].shape[0]
    tm = _pick(rows, (1024, 512, 256, 128))
    r_pad = jnp.pad(router, ((0, 0), (0, LANES - router.shape[1])))
    per_group = pl.BlockSpec((1, 1, D), lambda i: ((i * tm) // rows, 0, 0))
    return pl.pallas_call(
        _route_kernel,
        grid=(M // tm,),
        in_specs=[pl.BlockSpec((tm, D), lambda i: (i, 0)), per_group, per_group,
                  pl.BlockSpec((D, LANES), lambda i: (0, 0))],
        out_specs=[pl.BlockSpec((tm, D), lambda i: (i, 0)), pl.BlockSpec((tm, LANES), lambda i: (i, 0))],
        out_shape=[jax.ShapeDtypeStruct((M, D), BF16), jax.ShapeDtypeStruct((M, LANES), F32)],
        compiler_params=_cparams(("parallel",)),
        name="moe_route",
    )(x, mod[0], mod[1], r_pad)


def _combine_ln_kernel(x_ref, y0_ref, y1_ref, gate_ref, g_ref, b_ref, o_ref):
    y = y0_ref[...].astype(F32) + y1_ref[...].astype(F32)
    o_ref[...] = _ln_rows(DEEPNORM_ALPHA * x_ref[...] + gate_ref[0] * y, g_ref[...], b_ref[...])


def _combine_ln(x, y0, y1, gate, ln_g, ln_b):
    M, D = x.shape
    rows = M // gate.shape[0]
    tm = _pick(rows, (1024, 512, 256, 128))
    tile = pl.BlockSpec((tm, D), lambda i: (i, 0))
    row_vec = pl.BlockSpec((1, D), lambda i: (0, 0))
    return pl.pallas_call(
        _combine_ln_kernel,
        grid=(M // tm,),
        in_specs=[tile, tile, tile, pl.BlockSpec((1, 1, D), lambda i: ((i * tm) // rows, 0, 0)), row_vec, row_vec],
        out_specs=tile,
        out_shape=jax.ShapeDtypeStruct((M, D), F32),
        compiler_params=_cparams(("parallel",)),
        name="moe_combine_ln",
    )(x, y0, y1, gate, ln_g, ln_b)


def _moe_block(x, mod, gate, ln_g, ln_b, router, w_gu, w_down):
    B, L, D = x.shape
    M = B * L
    tm = MOE_TILE
    xf = x.reshape(M, D)
    hb, logits = _route(xf, mod, router)
    top_val, top_idx = lax.top_k(logits[:, :N_EXPERTS], TOP_K)
    weights = jax.nn.softmax(top_val, axis=-1)
    e_flat = top_idx.reshape(-1).astype(jnp.int32)
    onehot = (e_flat[:, None] == jnp.arange(N_EXPERTS, dtype=jnp.int32)[None, :]).astype(jnp.int32)
    rank = jnp.sum((jnp.cumsum(onehot, axis=0) - onehot) * onehot, axis=1)
    counts = jnp.sum(onehot, axis=0)
    padded = ((counts + tm - 1) // tm) * tm
    ends = jnp.cumsum(padded)
    pos = (ends - padded)[e_flat] + rank
    P = M * TOP_K + N_EXPERTS * tm
    slot = jnp.zeros((P,), jnp.int32).at[pos].set(jnp.arange(M * TOP_K, dtype=jnp.int32))
    tile_expert = jnp.minimum(jnp.searchsorted(ends, jnp.arange(P // tm, dtype=jnp.int32) * tm, side='right'),
                              N_EXPERTS - 1).astype(jnp.int32)
    xs = hb[slot // TOP_K]
    row_w = weights.reshape(-1)[slot][:, None]
    ys = _ffn(xs, w_gu.astype(BF16), w_down.astype(BF16), tile_expert, tm, row_scale=row_w, out_dtype=BF16)
    pos2 = pos.reshape(M, TOP_K)
    return _combine_ln(xf, ys[pos2[:, 0]], ys[pos2[:, 1]], gate, ln_g, ln_b).reshape(B, L, D)


def kernel(x, c, ctx, c_ctx, ada_w, ada_b, ln_g, ln_b,
           gla_w_in, gla_gate_w1, gla_gate_w2, gla_gate_b, gla_norm_g, gla_w_out,
           dn_w_in, dn_conv, dn_A_log, dn_dt_bias, dn_norm_g, dn_w_out,
           hy_w_in, hy_conv, hy_pe_w1, hy_pe_b1, hy_freq, hy_pe_w2, hy_pe_b2, hy_pe_w3,
           hy_decay, hy_skip, hy_w_out,
           ffn_w_gu, ffn_w_down, moe_router, moe_w_gu, moe_w_down):
    B, L, D = x.shape
    x = x + _grid_sincos(L // GRID_W, x.dtype)[None]
    xc = ctx
    cond_x = jax.nn.silu(c)
    cond_c = jax.nn.silu(c_ctx)[None, :]

    def project_out(y, w_out, res, gate, g, b):
        Bn, Ln, _ = res.shape
        return _mm(y.reshape(Bn * Ln, -1), w_out, ln=(res.reshape(Bn * Ln, D), gate, g, b)).reshape(Bn, Ln, D)

    for i in range(DEPTH):
        last = i == DEPTH - 1
        kind = i % N_MIXERS
        j = i // N_MIXERS
        mx = (jnp.dot(cond_x, ada_w[i], precision=HIGHEST) + ada_b[i]).reshape(B, 6, 1, D)
        mc = (jnp.dot(cond_c, ada_w[i], precision=HIGHEST) + ada_b[i]).reshape(1, 6, 1, D)
        sh1, sc1, g1, sh2, sc2, g2 = (mx[:, t] for t in range(6))
        csh1, csc1, cg1, csh2, csc2, cg2 = (mc[:, t] for t in range(6))
        ln1 = (ln_g[i, 0][None, :], ln_b[i, 0][None, :])
        ln2 = (ln_g[i, 1][None, :], ln_b[i, 1][None, :])
        if kind == 0:
            y_c, y_x = _gla_mixer(xc, (csc1, csh1), x, (sc1, sh1), gla_w_in[j], gla_gate_w1[j], gla_gate_w2[j],
                                  gla_gate_b[j], gla_norm_g[j])
            w_out = gla_w_out[j]
        elif kind == 1:
            y_c, y_x = _delta_mixer(xc, (csc1, csh1), x, (sc1, sh1), dn_w_in[j], dn_conv[j], dn_A_log[j],
                                    dn_dt_bias[j], dn_norm_g[j])
            w_out = dn_w_out[j]
        else:
            filt = (hy_pe_w1[j], hy_pe_b1[j], hy_freq[j], hy_pe_w2[j], hy_pe_b2[j], hy_pe_w3[j], hy_decay[j])
            y_x = _hyena_seq(x, (sc1, sh1), hy_w_in[j], hy_conv[j], filt, hy_skip[j])
            y_c = None if last else _hyena_seq(xc, (csc1, csh1), hy_w_in[j], hy_conv[j], filt, hy_skip[j])
            w_out = hy_w_out[j]
        x = project_out(y_x, w_out, x, g1, *ln1)
        if not last:
            xc = project_out(y_c, w_out, xc, cg1, *ln1)

        if i % 2 == 0:
            x = _dense_block(x, (sc2, sh2), g2, *ln2, ffn_w_gu[i // 2], ffn_w_down[i // 2])
            if not last:
                xc = _dense_block(xc, (csc2, csh2), cg2, *ln2, ffn_w_gu[i // 2], ffn_w_down[i // 2])
        else:
            moe = (moe_router[i // 2], moe_w_gu[i // 2], moe_w_down[i // 2])
            x = _moe_block(x, (sc2, sh2), g2, *ln2, *moe)
            if not last:
                xc = _moe_block(xc, (csc2, csh2), cg2, *ln2, *moe)
    return x
```

```python
import functools
import math

import jax
import jax.numpy as jnp
from jax import lax
from jax.experimental import pallas as pl
from jax.experimental.pallas import tpu as pltpu

F32 = jnp.float32
BF16 = jnp.bfloat16
HIGHEST = lax.Precision.HIGHEST

D_MODEL = 1024
DEPTH = 4
GRID_W = 64
LN_EPS = 1e-5
NORM_EPS = 1e-6
DEEPNORM_ALPHA = (2.0 * DEPTH) ** 0.25
CHUNK = 64
N_MIXERS = 3
LANES = 128

GLA_HEADS = 4
GLA_DK = D_MODEL // (2 * GLA_HEADS)
GLA_DV = D_MODEL // GLA_HEADS
GLA_KEY = GLA_HEADS * GLA_DK
GLA_VAL = GLA_HEADS * GLA_DV
GLA_RANK = 16
GLA_TAU = 16.0

DN_HEADS = 8
DN_DK = D_MODEL // DN_HEADS
DN_DV = D_MODEL // DN_HEADS
DN_KEY = DN_HEADS * DN_DK
DN_VAL = DN_HEADS * DN_DV

HY_ORDER = 2
HY_SIDES = 2
HY_BANDS = 16
HY_SHIFT = 0.05

N_EXPERTS = 8
TOP_K = 2
MOE_TILE = 512

VMEM_LIMIT_BYTES = 56 * 1024 * 1024

_NT = (((1,), (1,)), ((), ()))
_TN = (((0,), (0,)), ((), ()))


def _cparams(sem):
    return pltpu.CompilerParams(dimension_semantics=sem, vmem_limit_bytes=VMEM_LIMIT_BYTES)


def _ln_rows(y, g, b):
    mu = jnp.mean(y, axis=-1, keepdims=True)
    yc = y - mu
    var = jnp.mean(yc * yc, axis=-1, keepdims=True)
    return yc * lax.rsqrt(var + LN_EPS) * g + b


def _mm_kernel(*refs, nk, mod, ln):
    refs = list(refs)
    a_ref, b_ref = refs[:2]
    o_ref = refs[-1]
    extra = refs[2:-1]
    a = a_ref[...]
    if mod:
        a = a * (1.0 + extra[0][0]) + extra[1][0]
        extra = extra[2:]
    part = jnp.dot(a.astype(BF16), b_ref[...], preferred_element_type=F32)
    if ln:
        res_ref, gate_ref, g_ref, bias_ref = extra
        o_ref[...] = _ln_rows(DEEPNORM_ALPHA * res_ref[...] + gate_ref[0] * part, g_ref[...], bias_ref[...])
    elif nk == 1:
        o_ref[...] = part.astype(o_ref.dtype)
    else:
        k = pl.program_id(2)

        @pl.when(k == 0)
        def _():
            o_ref[...] = part

        @pl.when(k > 0)
        def _():
            o_ref[...] += part


def _pick(n, cands):
    for c in cands:
        if n % c == 0:
            return c
    return n


def _lane_tile(n, cap):
    best = LANES
    for t in range(LANES, cap + 1, LANES):
        if n % t == 0:
            best = t
    return best


def _mm(a, w, mod=None, ln=None, out_dtype=F32):
    M, K = a.shape
    N = w.shape[1]
    n_pad = (-N) % LANES
    wb = w.astype(BF16)
    if n_pad:
        wb = jnp.pad(wb, ((0, 0), (0, n_pad)))
    Np = N + n_pad
    groups = mod[0].shape[0] if mod is not None else (ln[1].shape[0] if ln is not None else 1)
    rows = M // groups
    tm = _pick(rows, (1024, 512, 256, 128, 64, 32, 16, 8))
    tn = Np if ln is not None else _lane_tile(Np, 1408)
    tk = K if (mod is not None or ln is not None) else _pick(K, (1024, 512, 256, 128))
    nk = K // tk
    per_group = lambda width: pl.BlockSpec((1, 1, width), lambda i, j, k: ((i * tm) // rows, 0, 0))
    specs = [pl.BlockSpec((tm, tk), lambda i, j, k: (i, k)),
             pl.BlockSpec((tk, tn), lambda i, j, k: (k, j))]
    args = [a, wb]
    if mod is not None:
        specs += [per_group(K), per_group(K)]
        args += list(mod)
    if ln is not None:
        assert n_pad == 0
        row_vec = pl.BlockSpec((1, N), lambda i, j, k: (0, 0))
        specs += [pl.BlockSpec((tm, N), lambda i, j, k: (i, 0)), per_group(N), row_vec, row_vec]
        args += list(ln)
    return pl.pallas_call(
        functools.partial(_mm_kernel, nk=nk, mod=mod is not None, ln=ln is not None),
        grid=(M // tm, Np // tn, nk),
        in_specs=specs,
        out_specs=pl.BlockSpec((tm, tn), lambda i, j, k: (i, j)),
        out_shape=jax.ShapeDtypeStruct((M, Np), out_dtype),
        compiler_params=_cparams(("parallel", "parallel", "arbitrary")),
        name="mm",
    )(*args)


def _mm3(u, w, mod=None, out_dtype=F32):
    B, L, K = u.shape
    return _mm(u.reshape(B * L, K), w, mod=mod, out_dtype=out_dtype).reshape(B, L, -1)


def _ffn_kernel(*refs, nf, block, row_scale):
    refs = list(refs)
    u_ref, wg_ref, wu_ref, wd_ref = refs[1:5]
    o_ref, acc_ref = refs[-2:]
    extra = refs[5:-2]
    f = pl.program_id(1)

    @pl.when(f == 0)
    def _():
        acc_ref[...] = jnp.zeros_like(acc_ref)

    u = u_ref[...]
    if block:
        u = u * (1.0 + extra[0][0]) + extra[1][0]
    u = u.astype(BF16)
    g = jnp.dot(u, wg_ref[0], preferred_element_type=F32)
    up = jnp.dot(u, wu_ref[0], preferred_element_type=F32)
    act = (g * jax.nn.sigmoid(g) * up).astype(BF16)
    acc_ref[...] += jnp.dot(act, wd_ref[0], preferred_element_type=F32)

    @pl.when(f == nf - 1)
    def _():
        y = acc_ref[...]
        if row_scale:
            y = y * extra[0][...]
        if block:
            y = _ln_rows(DEEPNORM_ALPHA * u_ref[...] + extra[2][0] * y, extra[3][...], extra[4][...])
        o_ref[...] = y.astype(o_ref.dtype)


def _ffn(u, w_gu, w_down, tile_expert, tm, block=None, row_scale=None, out_dtype=F32):
    M, D = u.shape
    F = w_down.shape[1]
    tf = _pick(F, (1792, 1408, 512, 256, 128))
    nf = F // tf
    specs = [pl.BlockSpec((tm, D), lambda i, f, te: (i, 0)),
             pl.BlockSpec((1, D, tf), lambda i, f, te: (te[i], 0, f)),
             pl.BlockSpec((1, D, tf), lambda i, f, te: (te[i], 0, f + nf)),
             pl.BlockSpec((1, tf, D), lambda i, f, te: (te[i], f, 0))]
    args = [tile_expert, u, w_gu, w_gu, w_down]
    if block is not None:
        rows = M // block[0].shape[0]
        assert rows % tm == 0
        per_group = pl.BlockSpec((1, 1, D), lambda i, f, te: ((i * tm) // rows, 0, 0))
        row_vec = pl.BlockSpec((1, D), lambda i, f, te: (0, 0))
        specs += [per_group, per_group, per_group, row_vec, row_vec]
        args += list(block)
    if row_scale is not None:
        specs.append(pl.BlockSpec((tm, 1), lambda i, f, te: (i, 0)))
        args.append(row_scale)
    grid_spec = pltpu.PrefetchScalarGridSpec(
        num_scalar_prefetch=1,
        grid=(M // tm, nf),
        in_specs=specs,
        out_specs=pl.BlockSpec((tm, D), lambda i, f, te: (i, 0)),
        scratch_shapes=[pltpu.VMEM((tm, D), F32)])
    return pl.pallas_call(
        functools.partial(_ffn_kernel, nf=nf, block=block is not None, row_scale=row_scale is not None),
        grid_spec=grid_spec,
        out_shape=jax.ShapeDtypeStruct((M, D), out_dtype),
        compiler_params=_cparams(("parallel", "arbitrary")),
        name="ffn",
    )(*args)


def _grid_sincos(rows, dtype):
    r, col = jnp.meshgrid(jnp.arange(rows, dtype=F32), jnp.arange(GRID_W, dtype=F32), indexing='ij')
    n_freq = D_MODEL // 4
    omega = 1.0 / (10000.0 ** (jnp.arange(n_freq, dtype=F32) / n_freq))

    def emb(p):
        ang = p.reshape(-1)[:, None] * omega[None, :]
        return jnp.concatenate([jnp.sin(ang), jnp.cos(ang)], -1)

    return jnp.concatenate([emb(r), emb(col)], -1).astype(dtype)


def _bdot(a, b, dims=None):
    a = a.astype(BF16)
    b = b.astype(BF16)
    if dims is None:
        return jnp.dot(a, b, preferred_element_type=F32)
    return lax.dot_general(a, b, dims, preferred_element_type=F32)


GLA_CHUNK = 128
GLA_BPS = 2


def _scan_specs(nc, rev, widths_and_blocks):
    def cidx(j):
        return nc - 1 - j if rev else j
    return [pl.BlockSpec((GLA_BPS, GLA_CHUNK, w), functools.partial(lambda b, j, cb: (b, cidx(j), cb), cb=cb))
            for w, cb in widths_and_blocks]


def _head_readout(o_h, norm_g, gate):
    inv = lax.rsqrt(jnp.mean(o_h * o_h, axis=-1, keepdims=True) + NORM_EPS)
    gate = gate.astype(F32)
    return o_h * inv * norm_g * (gate * jax.nn.sigmoid(gate))


def _gla_kernel(*refs, rev, nc, with_prev, readout):
    refs = list(refs)
    q_ref, k_ref, v_ref, t_ref, w2_ref, gb_ref, s0_ref = refs[:7]
    o_ref, sfin_ref, s_scr = refs[-3:]
    extra = refs[7:-3]
    prev_ref = extra.pop(0) if with_prev else None
    gate_ref, ng_ref = extra if readout else (None, None)
    j = pl.program_id(1)

    @pl.when(j == 0)
    def _():
        s_scr[...] = s0_ref[...]

    row = lax.broadcasted_iota(jnp.int32, (GLA_CHUNK, GLA_CHUNK), 0)
    col = lax.broadcasted_iota(jnp.int32, (GLA_CHUNK, GLA_CHUNK), 1)
    incl = (col >= row) if rev else (col <= row)
    last = 0 if rev else GLA_CHUNK - 1
    mid = GLA_CHUNK // 2
    scale = GLA_DK ** -0.5
    chains = [(bi, h) for bi in range(GLA_BPS) for h in range(GLA_HEADS)]
    ks = {h: slice(h * GLA_DK, (h + 1) * GLA_DK) for h in range(GLA_HEADS)}
    vs = {h: slice(h * GLA_DV, (h + 1) * GLA_DV) for h in range(GLA_HEADS)}
    b_all = []
    for bi in range(GLA_BPS):
        z = _bdot(t_ref[bi], w2_ref[...]) + gb_ref[...]
        log_a = (jnp.minimum(z, 0.0) - jnp.log(1.0 + jnp.exp(-jnp.abs(z)))) * (1.0 / GLA_TAU)
        b_all.append(jnp.dot(incl.astype(F32), log_a, precision=HIGHEST, preferred_element_type=F32))
    bhs = [b_all[bi][:, ks[h]] for bi, h in chains]
    b_lasts = [bh[last:last + 1, :] for bh in bhs]
    b_mids = [bh[mid:mid + 1, :] for bh in bhs]
    qfs = [q_ref[bi, :, ks[h]].astype(F32) * scale for bi, h in chains]
    kfs = [k_ref[bi, :, ks[h]].astype(F32) for bi, h in chains]
    vbs = [v_ref[bi, :, vs[h]].astype(BF16) for bi, h in chains]
    sts = [s_scr[bi, h] for bi, h in chains]
    n = range(len(chains))
    q_piv = [(qfs[i] * jnp.exp(bhs[i] - b_mids[i])).astype(BF16) for i in n]
    k_piv = [(kfs[i] * jnp.exp(b_mids[i] - bhs[i])).astype(BF16) for i in n]
    scores = [jnp.where(incl, _bdot(q_piv[i], k_piv[i], _NT), 0.0) for i in n]
    inter = [_bdot(qfs[i] * jnp.exp(bhs[i]), sts[i], _NT) for i in n]
    intra = [_bdot(scores[i], vbs[i]) for i in n]
    kv = [_bdot(vbs[i], kfs[i] * jnp.exp(b_lasts[i] - bhs[i]), _TN) for i in n]
    for i, (bi, h) in enumerate(chains):
        s_scr[bi, h] = sts[i] * jnp.exp(b_lasts[i]) + kv[i]
        o_h = intra[i] + inter[i]
        if prev_ref is not None:
            o_h = o_h + prev_ref[bi, :, vs[h]]
        if readout:
            o_h = _head_readout(o_h, ng_ref[...], gate_ref[bi, :, vs[h]])
        o_ref[bi, :, vs[h]] = o_h.astype(o_ref.dtype)

    @pl.when(j == nc - 1)
    def _():
        sfin_ref[...] = s_scr[...]


def _gla_scan(proj, w2, gb, s0, prev, rev, norm_g=None):
    B, L, _ = proj.shape
    assert L % GLA_CHUNK == 0 and B % GLA_BPS == 0
    nc = L // GLA_CHUNK
    with_prev = prev is not None
    readout = norm_g is not None
    gate_cb = (2 * GLA_KEY + 2 * GLA_VAL) // LANES
    state = pl.BlockSpec((GLA_BPS, GLA_HEADS, GLA_DV, GLA_DK), lambda b, j: (b, 0, 0, 0))
    specs = _scan_specs(nc, rev, [(GLA_KEY, 0), (GLA_KEY, 1), (GLA_VAL, 1), (LANES, gate_cb)])
    specs += [pl.BlockSpec((LANES, GLA_KEY), lambda b, j: (0, 0)),
              pl.BlockSpec((1, GLA_KEY), lambda b, j: (0, 0)),
              state]
    args = [proj, proj, proj, proj, w2, gb, s0]
    o_spec = _scan_specs(nc, rev, [(GLA_VAL, 0)])[0]
    if with_prev:
        specs.append(o_spec)
        args.append(prev)
    if readout:
        specs += [_scan_specs(nc, rev, [(GLA_VAL, 2)])[0], pl.BlockSpec((1, GLA_DV), lambda b, j: (0, 0))]
        args += [proj, norm_g.reshape(1, GLA_DV)]
    return pl.pallas_call(
        functools.partial(_gla_kernel, rev=rev, nc=nc, with_prev=with_prev, readout=readout),
        grid=(B // GLA_BPS, nc),
        in_specs=specs,
        out_specs=[o_spec, state],
        out_shape=[jax.ShapeDtypeStruct((B, L, GLA_VAL), BF16 if readout else F32),
                   jax.ShapeDtypeStruct((B, GLA_HEADS, GLA_DV, GLA_DK), F32)],
        scratch_shapes=[pltpu.VMEM((GLA_BPS, GLA_HEADS, GLA_DV, GLA_DK), F32)],
        compiler_params=_cparams(("parallel", "arbitrary")),
        name="gla_scan_rev" if rev else "gla_scan_fwd",
    )(*args)


def _gla_mixer(x_c, mod_c, x_x, mod_x, w_in, gate_w1, gate_w2, gate_b, norm_g):
    w_ext = jnp.concatenate([w_in, gate_w1[0], gate_w1[1]], axis=1)
    w2 = []
    for d in range(2):
        w2.append(jnp.zeros((LANES, GLA_KEY), F32).at[d * GLA_RANK:(d + 1) * GLA_RANK].set(gate_w2[d]).astype(BF16))
    gb = gate_b[:, None, :]

    def bidir(proj, s0f, s0b):
        o, s_f = _gla_scan(proj, w2[0], gb[0], s0f, None, False)
        y, s_b = _gla_scan(proj, w2[1], gb[1], s0b, o, True, norm_g=norm_g)
        return y, s_f, s_b

    s0 = jnp.zeros((x_c.shape[0], GLA_HEADS, GLA_DV, GLA_DK), F32)
    y_c, s_f, s_b = bidir(_mm3(x_c, w_ext, mod=mod_c, out_dtype=BF16), s0, s0)
    y_x, _, _ = bidir(_mm3(x_x, w_ext, mod=mod_x, out_dtype=BF16), s_f, s_b)
    return y_c, y_x


def _dn_prep_kernel(x_ref, w_ref, o_ref, *, L):
    cb = pl.program_id(1)
    y = _dwconv3_rows(x_ref[0], w_ref[...])
    y = y * jax.nn.sigmoid(y)
    inv = lax.rsqrt(jnp.sum(y * y, axis=1, keepdims=True) + NORM_EPS)
    q_scale = jnp.where(cb < DN_HEADS, DN_DK ** -0.5, 1.0)
    fac = jnp.where(cb < 2 * DN_HEADS, inv * q_scale, 1.0)
    o_ref[0] = (y * fac).astype(BF16)


def _dn_prep(proj, conv_w):
    B, L, _ = proj.shape
    ncb = (2 * DN_KEY + DN_VAL) // LANES
    return pl.pallas_call(
        functools.partial(_dn_prep_kernel, L=L),
        grid=(B, ncb),
        in_specs=[pl.BlockSpec((1, L, LANES), lambda b, c: (b, 0, c)),
                  pl.BlockSpec((conv_w.shape[0], LANES), lambda b, c: (0, c))],
        out_specs=pl.BlockSpec((1, L, LANES), lambda b, c: (b, 0, c)),
        out_shape=jax.ShapeDtypeStruct((B, L, ncb * LANES), BF16),
        compiler_params=_cparams(("parallel", "parallel")),
        name="dn_prep",
    )(proj, conv_w)


DN_GROUP = 4
DN_ROWS = DN_GROUP * CHUNK


DN_CPS = 2
DN_BPS = 2
DN_LEVELS = CHUNK.bit_length() - 1
M_INCL, M_STRICT, M_EYE, M_LEVEL0 = 0, 1, 2, 3


def _dn_masks(rev):
    r = jnp.arange(DN_ROWS, dtype=jnp.int32)[:, None]
    c = jnp.arange(DN_ROWS, dtype=jnp.int32)[None, :]
    same_head = (r // CHUNK) == (c // CHUNK)
    rt, ct = r % CHUNK, c % CHUNK
    out = [same_head & ((ct >= rt) if rev else (ct <= rt)), same_head & ((ct > rt) if rev else (ct < rt)), r == c]
    for lv in range(DN_LEVELS):
        s = 1 << lv
        r_hi, c_hi = (r // s) % 2 == 1, (c // s) % 2 == 1
        out.append(((r // (2 * s)) == (c // (2 * s))) & ((c_hi & ~r_hi) if rev else (r_hi & ~c_hi)))
    return jnp.stack(out).astype(F32)


def _dn_chunk_kernel(q_ref, k_ref, v_ref, ba_ref, par_ref, m_ref, p_ref, qm_ref, g_ref, r_ref, oi_ref, *, rev):
    prow = lax.broadcasted_iota(jnp.int32, (2 * CHUNK, CHUNK), 0)
    pcol = lax.broadcasted_iota(jnp.int32, (2 * CHUNK, CHUNK), 1)
    tri_pad = (((pcol >= prow) if rev else (pcol <= prow)) & (prow < CHUNK)).astype(F32)
    d = 1 if rev else 0
    last = 0 if rev else CHUNK - 1
    incl = m_ref[M_INCL]
    eye = m_ref[M_EYE]
    items = []
    for ci in range(DN_CPS):
        tr = slice(ci * CHUNK, (ci + 1) * CHUNK)
        ba = ba_ref[0, tr, :]
        beta_all = jax.nn.sigmoid(ba)
        sp_in = ba + par_ref[1:2, :]
        softplus = jnp.maximum(sp_in, 0.0) + jnp.log(1.0 + jnp.exp(-jnp.abs(sp_in)))
        g_all = par_ref[0:1, :] * softplus
        gc_pad = jnp.dot(tri_pad, g_all, precision=HIGHEST, preferred_element_type=F32)
        gc = gc_pad[:CHUNK]
        gc_t = gc_pad.T
        for grp in range(DN_HEADS // DN_GROUP):
            heads = range(grp * DN_GROUP, (grp + 1) * DN_GROUP)
            gls = [2 * DN_HEADS + d * DN_HEADS + h for h in heads]
            qs = jnp.concatenate([q_ref[0, tr, h * DN_DK:(h + 1) * DN_DK] for h in heads], axis=0)
            ks = jnp.concatenate([k_ref[0, tr, h * DN_DK:(h + 1) * DN_DK] for h in heads], axis=0)
            vs = jnp.concatenate([v_ref[0, tr, h * DN_DV:(h + 1) * DN_DV] for h in heads], axis=0)
            beta = jnp.concatenate([beta_all[:, d * DN_HEADS + h:d * DN_HEADS + h + 1] for h in heads], axis=0)
            gcol = jnp.concatenate([gc[:, gl:gl + 1] for gl in gls], axis=0)
            glast = jnp.concatenate([jnp.broadcast_to(gc[last:last + 1, gl:gl + 1], (CHUNK, 1)) for gl in gls],
                                    axis=0)
            rows = [gc_t[gl:gl + 1, :] for gl in gls]
            grow = jnp.concatenate([rows[0] + pltpu.roll(rows[1], CHUNK, 1),
                                    rows[2] + pltpu.roll(rows[3], CHUNK, 1)], axis=1)
            items.append(dict(ci=ci, tr=tr, heads=heads, qs=qs, ks=ks, vs=vs, beta=beta, gcol=gcol, glast=glast,
                              grow=grow, gfin=[gc[last:last + 1, gl:gl + 1] for gl in gls]))
    dms = [jnp.exp(jnp.minimum(it["gcol"] - it["grow"], 0.0)) * incl for it in items]
    kfs = [it["ks"].astype(F32) for it in items]
    kbetas = [kf * it["beta"] for kf, it in zip(kfs, items)]
    a_s = [_bdot(kb, it["ks"], _NT) * (dm * m_ref[M_STRICT]) for kb, it, dm in zip(kbetas, items, dms)]
    ts = [eye - a * m_ref[M_LEVEL0] for a in a_s]
    for lv in range(1, DN_LEVELS):
        mts = [_bdot(a * m_ref[M_LEVEL0 + lv], t) for a, t in zip(a_s, ts)]
        ts = [t - _bdot(t, mt) for t, mt in zip(ts, mts)]
    rhs = [jnp.concatenate([it["vs"].astype(F32) * it["beta"], kb * jnp.exp(it["gcol"])], axis=1)
           for it, kb in zip(items, kbetas)]
    sols = [_bdot(t, r) for t, r in zip(ts, rhs)]
    qks = [_bdot(it["qs"], it["ks"], _NT) * dm for it, dm in zip(items, dms)]
    qsols = [_bdot(qk, sol) for qk, sol in zip(qks, sols)]
    for it, kf, sol, qsol in zip(items, kfs, sols, qsols):
        ci, tr = it["ci"], it["tr"]
        k_dec = kf * jnp.exp(it["glast"] - it["gcol"])
        r_all = it["qs"].astype(F32) * jnp.exp(it["gcol"]) - qsol[:, DN_DV:]
        for i, h in enumerate(it["heads"]):
            rs = slice(i * CHUNK, (i + 1) * CHUNK)
            hs = slice(h * DN_DV, (h + 1) * DN_DV)
            pq = _bdot(k_dec[rs], sol[rs], _TN)
            qm_ref[0, ci, h] = pq[:, :DN_DV]
            p_ref[0, ci, h] = (-pq[:, DN_DV:]).astype(BF16)
            g_ref[0, ci, h:h + 1, :] = jnp.broadcast_to(jnp.exp(it["gfin"][i]), (1, DN_DV))
            r_ref[0, tr, hs] = r_all[rs].astype(BF16)
            oi_ref[0, tr, hs] = qsol[rs, :DN_DV]


def _dn_chunk(qkv, proj, par, rev):
    B, L, _ = qkv.shape
    nc = L // CHUNK
    assert nc % DN_CPS == 0
    ba_cb = (2 * DN_KEY + 2 * DN_VAL) // LANES
    tok = lambda w, cb: pl.BlockSpec((1, DN_CPS * CHUNK, w), lambda b, j: (b, j, cb))
    mat = pl.BlockSpec((1, DN_CPS, DN_HEADS, DN_DK, DN_DV), lambda b, j: (b, j, 0, 0, 0))
    masks = _dn_masks(rev)
    return pl.pallas_call(
        functools.partial(_dn_chunk_kernel, rev=rev),
        grid=(B, nc // DN_CPS),
        in_specs=[tok(DN_KEY, 0), tok(DN_KEY, 1), tok(DN_VAL, 2), tok(LANES, ba_cb),
                  pl.BlockSpec((2, LANES), lambda b, j: (0, 0)),
                  pl.BlockSpec(masks.shape, lambda b, j: (0, 0, 0), pipeline_mode=pl.Buffered(1))],
        out_specs=[mat, mat, pl.BlockSpec((1, DN_CPS, DN_HEADS, DN_DV), lambda b, j: (b, j, 0, 0)),
                   tok(DN_KEY, 0), tok(DN_VAL, 0)],
        out_shape=[jax.ShapeDtypeStruct((B, nc, DN_HEADS, DN_DK, DN_DV), BF16),
                   jax.ShapeDtypeStruct((B, nc, DN_HEADS, DN_DK, DN_DV), F32),
                   jax.ShapeDtypeStruct((B, nc, DN_HEADS, DN_DV), F32),
                   jax.ShapeDtypeStruct((B, L, DN_KEY), BF16),
                   jax.ShapeDtypeStruct((B, L, DN_VAL), F32)],
        compiler_params=_cparams(("parallel", "parallel")),
        name="dn_chunk_rev" if rev else "dn_chunk_fwd",
    )(qkv, qkv, qkv, proj, par, masks)


def _dn_seq_kernel(*refs, nc, with_prev, readout):
    refs = list(refs)
    p_ref, qm_ref, g_ref, r_ref, oi_ref, s0_ref = refs[:6]
    o_ref, sfin_ref, s_scr = refs[-3:]
    extra = refs[6:-3]
    prev_ref = extra.pop(0) if with_prev else None
    gate_ref, ng_ref = extra if readout else (None, None)
    j = pl.program_id(1)

    @pl.when(j == 0)
    def _():
        s_scr[...] = s0_ref[...]

    chains = [(bi, h) for bi in range(DN_BPS) for h in range(DN_HEADS)]
    states = [s_scr[bi, h] for bi, h in chains]
    xs = []
    for i, (bi, h) in enumerate(chains):
        lhs = jnp.concatenate([p_ref[bi, 0, h], r_ref[bi, :, h * DN_DK:(h + 1) * DN_DK]], axis=0)
        xs.append(_bdot(lhs, states[i]))
    for i, (bi, h) in enumerate(chains):
        hs = slice(h * DN_DV, (h + 1) * DN_DV)
        s_scr[bi, h] = g_ref[bi, 0, h:h + 1, :] * states[i] + xs[i][:DN_DK] + qm_ref[bi, 0, h]
        o_h = xs[i][DN_DK:] + oi_ref[bi, :, hs]
        if prev_ref is not None:
            o_h = o_h + prev_ref[bi, :, hs]
        if readout:
            o_h = _head_readout(o_h, ng_ref[...], gate_ref[bi, :, hs])
        o_ref[bi, :, hs] = o_h.astype(o_ref.dtype)

    @pl.when(j == nc - 1)
    def _():
        sfin_ref[...] = s_scr[...]


def _dn_seq(chunk_terms, s0, prev, rev, readout=None):
    p, qm, g, r, oi = chunk_terms
    B, L, _ = r.shape
    nc = L // CHUNK
    with_prev = prev is not None

    def cidx(j):
        return nc - 1 - j if rev else j

    assert B % DN_BPS == 0
    mat = pl.BlockSpec((DN_BPS, 1, DN_HEADS, DN_DK, DN_DV), lambda b, j: (b, cidx(j), 0, 0, 0))
    tok = pl.BlockSpec((DN_BPS, CHUNK, DN_VAL), lambda b, j: (b, cidx(j), 0))
    state = pl.BlockSpec((DN_BPS, DN_HEADS, DN_DK, DN_DV), lambda b, j: (b, 0, 0, 0))
    specs = [mat, mat, pl.BlockSpec((DN_BPS, 1, DN_HEADS, DN_DV), lambda b, j: (b, cidx(j), 0, 0)), tok, tok, state]
    args = [p, qm, g, r, oi, s0]
    if with_prev:
        specs.append(tok)
        args.append(prev)
    if readout is not None:
        z_cb = (2 * DN_KEY + DN_VAL) // DN_VAL
        specs += [pl.BlockSpec((DN_BPS, CHUNK, DN_VAL), lambda b, j: (b, cidx(j), z_cb)),
                  pl.BlockSpec((1, DN_DV), lambda b, j: (0, 0))]
        args += [readout[0], readout[1].reshape(1, DN_DV)]
    return pl.pallas_call(
        functools.partial(_dn_seq_kernel, nc=nc, with_prev=with_prev, readout=readout is not None),
        grid=(B // DN_BPS, nc),
        in_specs=specs,
        out_specs=[tok, state],
        out_shape=[jax.ShapeDtypeStruct((B, L, DN_VAL), BF16 if readout is not None else F32),
                   jax.ShapeDtypeStruct((B, DN_HEADS, DN_DK, DN_DV), F32)],
        scratch_shapes=[pltpu.VMEM((DN_BPS, DN_HEADS, DN_DK, DN_DV), F32)],
        compiler_params=_cparams(("parallel", "arbitrary")),
        name="dn_seq_rev" if rev else "dn_seq_fwd",
    )(*args)


def _delta_mixer(x_c, mod_c, x_x, mod_x, w_in, conv_w, A_log, dt_bias, norm_g):
    a_lanes = slice(2 * DN_HEADS, 4 * DN_HEADS)
    par = jnp.zeros((2, LANES), F32)
    par = par.at[0, a_lanes].set(-jnp.exp(A_log).reshape(-1)).at[1, a_lanes].set(dt_bias.reshape(-1))

    def bidir(proj, s0f, s0b):
        qkv = _dn_prep(proj, conv_w)
        o, s_f = _dn_seq(_dn_chunk(qkv, proj, par, False), s0f, None, False)
        y, s_b = _dn_seq(_dn_chunk(qkv, proj, par, True), s0b, o, True, readout=(proj, norm_g))
        return y, s_f, s_b

    s0 = jnp.zeros((x_c.shape[0], DN_HEADS, DN_DK, DN_DV), F32)
    y_c, s_f, s_b = bidir(_mm3(x_c, w_in, mod=mod_c), s0, s0)
    y_x, _, _ = bidir(_mm3(x_x, w_in, mod=mod_x), s_f, s_b)
    return y_c, y_x


def _hyena_filter_spectrum(L, pe_w1, pe_b1, freq, pe_w2, pe_b2, pe_w3, decay):
    t = jnp.arange(L, dtype=F32)
    tn = t / L
    ang = (2.0 * math.pi / L) * t[:, None] * jnp.arange(1, HY_BANDS + 1, dtype=F32)[None, :]
    feats = jnp.concatenate([tn[:, None], jnp.cos(ang), jnp.sin(ang)], -1)
    h = jnp.sin(freq * (feats @ pe_w1 + pe_b1))
    h = jnp.sin(freq * (h @ pe_w2 + pe_b2))
    h = (h @ pe_w3) * (jnp.exp(-tn[:, None] * decay) + HY_SHIFT)
    h = h.reshape(L, HY_ORDER, HY_SIDES, D_MODEL)
    h_circ = jnp.concatenate([h[:, :, 0], jnp.zeros((1, HY_ORDER, D_MODEL), F32), h[:0:-1, :, 1]], axis=0)
    return jnp.fft.rfft(h_circ, axis=0)


HY_TD = 256
HY_TK = 512


def _dwconv3_rows(x, w):
    L = x.shape[0]
    rowi = lax.broadcasted_iota(jnp.int32, x.shape, 0)
    x_prev = jnp.where(rowi == 0, 0.0, pltpu.roll(x, 1, 0))
    x_next = jnp.where(rowi == L - 1, 0.0, pltpu.roll(x, L - 1, 0))
    return w[0:1] * x_prev + w[1:2] * x + w[2:3] * x_next


def _hy_tables(L):
    n = 2 * L
    k = jnp.arange(L, dtype=jnp.int32)
    ang = (2.0 * math.pi / n) * ((k[:, None] * k[None, :]) % n).astype(F32)
    sign = 1.0 - 2.0 * (k % 2).astype(F32)
    ws = (-jnp.sin(ang)).at[0, :].set(sign)
    return jnp.cos(ang).astype(BF16), ws.astype(BF16), ws.T.astype(BF16)


def _hy_filter(L, filt):
    spec = _hyena_filter_spectrum(L, *filt)
    n = 2 * L
    scale = jnp.full((L, 1, 1), 2.0 / n, F32).at[0].set(1.0 / n)
    hr = jnp.real(spec[:L]) * scale
    hi = (jnp.imag(spec[:L]) * scale).at[0].set(0.0)
    hr2 = hr.at[0].set(jnp.real(spec[L]) / n)
    return tuple(jnp.transpose(a, (1, 0, 2)) for a in (hr, hi, hr2))


def _hy_fwd_kernel(x_ref, cw_ref, c_ref, ws_ref, hr_ref, hi_ref, hr2_ref, yr_ref, yi_ref, *, L, conv):
    x = x_ref[0].astype(F32)
    vb = (_dwconv3_rows(x, cw_ref[...]) if conv else x).astype(BF16)
    tk = min(HY_TK, L)
    for kb in range(L // tk):
        rows = slice(kb * tk, (kb + 1) * tk)
        ur = jnp.dot(c_ref[rows, :], vb, preferred_element_type=F32)
        ui = jnp.dot(ws_ref[rows, :], vb, preferred_element_type=F32)
        hr = hr_ref[rows, :]
        hi = hi_ref[rows, :]
        yr_ref[0, rows, :] = (ur * hr - ui * hi).astype(BF16)
        yi_ref[0, rows, :] = (ur * hi + ui * hr2_ref[rows, :]).astype(BF16)


def _hy_inv_kernel(yr_ref, yi_ref, c_ref, wst_ref, g_ref, gw_ref, v_ref, vw_ref, sk_ref, o_ref, *, L, conv_v):
    gate = _dwconv3_rows(g_ref[0].astype(F32), gw_ref[...])
    v = v_ref[0].astype(F32)
    if conv_v:
        v = _dwconv3_rows(v, vw_ref[...])
    yr = yr_ref[0]
    yi = yi_ref[0]
    tk = min(HY_TK, L)
    for tb in range(L // tk):
        rows = slice(tb * tk, (tb + 1) * tk)
        y = (jnp.dot(c_ref[rows, :], yr, preferred_element_type=F32)
             + jnp.dot(wst_ref[rows, :], yi, preferred_element_type=F32))
        o_ref[0, rows, :] = (gate[rows, :] * (y + sk_ref[...] * v[rows, :])).astype(o_ref.dtype)


def _hy_specs(L, td):
    table = pl.BlockSpec((L, L), lambda d, b: (0, 0), pipeline_mode=pl.Buffered(1))
    col = lambda off: pl.BlockSpec((1, L, td), functools.partial(lambda d, b, off: (b, 0, off + d), off=off))
    par = lambda r: pl.BlockSpec((r, td), lambda d, b: (0, d))
    return table, col, par


def _hy_fwd(x, col_off, conv_w, c, ws, hr, hi, hr2):
    B, L, _ = x.shape
    D = hr.shape[1]
    td = HY_TD
    table, col, par = _hy_specs(L, td)
    conv = conv_w is not None
    if not conv:
        conv_w = jnp.zeros((3, D), F32)
    out = jax.ShapeDtypeStruct((B, L, D), BF16)
    return pl.pallas_call(
        functools.partial(_hy_fwd_kernel, L=L, conv=conv),
        grid=(D // td, B),
        in_specs=[col(col_off), par(3), table, table, par(L), par(L), par(L)],
        out_specs=[col(0), col(0)],
        out_shape=[out, out],
        compiler_params=_cparams(("parallel", "parallel")),
        name="hy_fwd",
    )(x, conv_w, c, ws, hr, hi, hr2)


def _hy_inv(yr, yi, c, wst, proj, gate_off, gate_w, sig, sig_off, sig_w, skip, out_dtype=F32):
    B, L, D = yr.shape
    td = HY_TD
    table, col, par = _hy_specs(L, td)
    conv_v = sig_w is not None
    if not conv_v:
        sig_w = jnp.zeros((3, D), F32)
    return pl.pallas_call(
        functools.partial(_hy_inv_kernel, L=L, conv_v=conv_v),
        grid=(D // td, B),
        in_specs=[col(0), col(0), table, table, col(gate_off), par(3), col(sig_off), par(3), par(1)],
        out_specs=col(0),
        out_shape=jax.ShapeDtypeStruct((B, L, D), out_dtype),
        compiler_params=_cparams(("parallel", "parallel")),
        name="hy_inv",
    )(yr, yi, c, wst, proj, gate_w, sig, sig_w, skip)


def _hyena_seq(x, mod, w_in, conv_w, filt, skip):
    B, L, D = x.shape
    assert conv_w.shape[0] == 3 and D % HY_TD == 0
    nd = D // HY_TD
    proj = _mm3(x, w_in, mod=mod, out_dtype=BF16)
    cw = [conv_w[:, i * D:(i + 1) * D] for i in range(3)]
    c, ws, wst = _hy_tables(L)
    hr, hi, hr2 = _hy_filter(L, filt)
    yr, yi = _hy_fwd(proj, 2 * nd, cw[2], c, ws, hr[0], hi[0], hr2[0])
    z = _hy_inv(yr, yi, c, wst, proj, 0, cw[0], proj, 2 * nd, cw[2], skip[0:1])
    yr, yi = _hy_fwd(z, 0, None, c, ws, hr[1], hi[1], hr2[1])
    return _hy_inv(yr, yi, c, wst, proj, nd, cw[1], z, 0, None, skip[1:2], out_dtype=BF16)


def _dense_block(x, mod, gate, ln_g, ln_b, w_gu, w_down):
    B, L, D = x.shape
    M = B * L
    tm = _pick(M // mod[0].shape[0], (512, 256, 128))
    te = jnp.zeros((M // tm,), jnp.int32)
    block = (mod[0], mod[1], gate, ln_g, ln_b)
    return _ffn(x.reshape(M, D), w_gu.astype(BF16)[None], w_down.astype(BF16)[None], te, tm,
                block=block).reshape(B, L, D)


def _route_kernel(x_ref, sc_ref, sh_ref, r_ref, h_ref, lg_ref):
    h = x_ref[...] * (1.0 + sc_ref[0]) + sh_ref[0]
    h_ref[...] = h.astype(BF16)
    lg_ref[...] = jnp.dot(h, r_ref[...], precision=HIGHEST, preferred_element_type=F32)


def _route(x, mod, router):
    M, D = x.shape
    rows = M // mod[0].shape[0]
    tm = _pick(rows, (1024, 512, 256, 128))
    r_pad = jnp.pad(router, ((0, 0), (0, LANES - router.shape[1])))
    per_group = pl.BlockSpec((1, 1, D), lambda i: ((i * tm) // rows, 0, 0))
    return pl.pallas_call(
        _route_kernel,
        grid=(M // tm,),
        in_specs=[pl.BlockSpec((tm, D), lambda i: (i, 0)), per_group, per_group,
                  pl.BlockSpec((D, LANES), lambda i: (0, 0))],
        out_specs=[pl.BlockSpec((tm, D), lambda i: (i, 0)), pl.BlockSpec((tm, LANES), lambda i: (i, 0))],
        out_shape=[jax.ShapeDtypeStruct((M, D), BF16), jax.ShapeDtypeStruct((M, LANES), F32)],
        compiler_params=_cparams(("parallel",)),
        name="moe_route",
    )(x, mod[0], mod[1], r_pad)


def _combine_ln_kernel(x_ref, y0_ref, y1_ref, gate_ref, g_ref, b_ref, o_ref):
    y = y0_ref[...].astype(F32) + y1_ref[...].astype(F32)
    o_ref[...] = _ln_rows(DEEPNORM_ALPHA * x_ref[...] + gate_ref[0] * y, g_ref[...], b_ref[...])


def _combine_ln(x, y0, y1, gate, ln_g, ln_b):
    M, D = x.shape
    rows = M // gate.shape[0]
    tm = _pick(rows, (1024, 512, 256, 128))
    tile = pl.BlockSpec((tm, D), lambda i: (i, 0))
    row_vec = pl.BlockSpec((1, D), lambda i: (0, 0))
    return pl.pallas_call(
        _combine_ln_kernel,
        grid=(M // tm,),
        in_specs=[tile, tile, tile, pl.BlockSpec((1, 1, D), lambda i: ((i * tm) // rows, 0, 0)), row_vec, row_vec],
        out_specs=tile,
        out_shape=jax.ShapeDtypeStruct((M, D), F32),
        compiler_params=_cparams(("parallel",)),
        name="moe_combine_ln",
    )(x, y0, y1, gate, ln_g, ln_b)


def _moe_block(x, mod, gate, ln_g, ln_b, router, w_gu, w_down):
    B, L, D = x.shape
    M = B * L
    tm = MOE_TILE
    xf = x.reshape(M, D)
    hb, logits = _route(xf, mod, router)
    top_val, top_idx = lax.top_k(logits[:, :N_EXPERTS], TOP_K)
    weights = jax.nn.softmax(top_val, axis=-1)
    e_flat = top_idx.reshape(-1).astype(jnp.int32)
    onehot = (e_flat[:, None] == jnp.arange(N_EXPERTS, dtype=jnp.int32)[None, :]).astype(jnp.int32)
    rank = jnp.sum((jnp.cumsum(onehot, axis=0) - onehot) * onehot, axis=1)
    counts = jnp.sum(onehot, axis=0)
    padded = ((counts + tm - 1) // tm) * tm
    ends = jnp.cumsum(padded)
    pos = (ends - padded)[e_flat] + rank
    P = M * TOP_K + N_EXPERTS * tm
    slot = jnp.zeros((P,), jnp.int32).at[pos].set(jnp.arange(M * TOP_K, dtype=jnp.int32))
    tile_expert = jnp.minimum(jnp.searchsorted(ends, jnp.arange(P // tm, dtype=jnp.int32) * tm, side='right'),
                              N_EXPERTS - 1).astype(jnp.int32)
    xs = hb[slot // TOP_K]
    row_w = weights.reshape(-1)[slot][:, None]
    ys = _ffn(xs, w_gu.astype(BF16), w_down.astype(BF16), tile_expert, tm, row_scale=row_w, out_dtype=BF16)
    pos2 = pos.reshape(M, TOP_K)
    return _combine_ln(xf, ys[pos2[:, 0]], ys[pos2[:, 1]], gate, ln_g, ln_b).reshape(B, L, D)


def kernel(x, c, ctx, c_ctx, ada_w, ada_b, ln_g, ln_b,
           gla_w_in, gla_gate_w1, gla_gate_w2, gla_gate_b, gla_norm_g, gla_w_out,
           dn_w_in, dn_conv, dn_A_log, dn_dt_bias, dn_norm_g, dn_w_out,
           hy_w_in, hy_conv, hy_pe_w1, hy_pe_b1, hy_freq, hy_pe_w2, hy_pe_b2, hy_pe_w3,
           hy_decay, hy_skip, hy_w_out,
           ffn_w_gu, ffn_w_down, moe_router, moe_w_gu, moe_w_down):
    B, L, D = x.shape
    x = x + _grid_sincos(L // GRID_W, x.dtype)[None]
    xc = ctx
    cond_x = jax.nn.silu(c)
    cond_c = jax.nn.silu(c_ctx)[None, :]

    def project_out(y, w_out, res, gate, g, b):
        Bn, Ln, _ = res.shape
        return _mm(y.reshape(Bn * Ln, -1), w_out, ln=(res.reshape(Bn * Ln, D), gate, g, b)).reshape(Bn, Ln, D)

    for i in range(DEPTH):
        last = i == DEPTH - 1
        kind = i % N_MIXERS
        j = i // N_MIXERS
        mx = (jnp.dot(cond_x, ada_w[i], precision=HIGHEST) + ada_b[i]).reshape(B, 6, 1, D)
        mc = (jnp.dot(cond_c, ada_w[i], precision=HIGHEST) + ada_b[i]).reshape(1, 6, 1, D)
        sh1, sc1, g1, sh2, sc2, g2 = (mx[:, t] for t in range(6))
        csh1, csc1, cg1, csh2, csc2, cg2 = (mc[:, t] for t in range(6))
        ln1 = (ln_g[i, 0][None, :], ln_b[i, 0][None, :])
        ln2 = (ln_g[i, 1][None, :], ln_b[i, 1][None, :])
        if kind == 0:
            y_c, y_x = _gla_mixer(xc, (csc1, csh1), x, (sc1, sh1), gla_w_in[j], gla_gate_w1[j], gla_gate_w2[j],
                                  gla_gate_b[j], gla_norm_g[j])
            w_out = gla_w_out[j]
        elif kind == 1:
            y_c, y_x = _delta_mixer(xc, (csc1, csh1), x, (sc1, sh1), dn_w_in[j], dn_conv[j], dn_A_log[j],
                                    dn_dt_bias[j], dn_norm_g[j])
            w_out = dn_w_out[j]
        else:
            filt = (hy_pe_w1[j], hy_pe_b1[j], hy_freq[j], hy_pe_w2[j], hy_pe_b2[j], hy_pe_w3[j], hy_decay[j])
            y_x = _hyena_seq(x, (sc1, sh1), hy_w_in[j], hy_conv[j], filt, hy_skip[j])
            y_c = None if last else _hyena_seq(xc, (csc1, csh1), hy_w_in[j], hy_conv[j], filt, hy_skip[j])
            w_out = hy_w_out[j]
        x = project_out(y_x, w_out, x, g1, *ln1)
        if not last:
            xc = project_out(y_c, w_out, xc, cg1, *ln1)

        if i % 2 == 0:
            x = _dense_block(x, (sc2, sh2), g2, *ln2, ffn_w_gu[i // 2], ffn_w_down[i // 2])
            if not last:
                xc = _dense_block(xc, (csc2, csh2), cg2, *ln2, ffn_w_gu[i // 2], ffn_w_down[i // 2])
        else:
            moe = (moe_router[i // 2], moe_w_gu[i // 2], moe_w_down[i // 2])
            x = _moe_block(x, (sc2, sh2), g2, *ln2, *moe)
            if not last:
                xc = _moe_block(xc, (csc2, csh2), cg2, *ln2, *moe)
    return x
```

```python
import functools
import math

import jax
import jax.numpy as jnp
from jax import lax
from jax.experimental import pallas as pl
from jax.experimental.pallas import tpu as pltpu

F32 = jnp.float32
BF16 = jnp.bfloat16
HIGHEST = lax.Precision.HIGHEST

D_MODEL = 1024
DEPTH = 4
GRID_W = 64
LN_EPS = 1e-5
NORM_EPS = 1e-6
DEEPNORM_ALPHA = (2.0 * DEPTH) ** 0.25
CHUNK = 64
N_MIXERS = 3
LANES = 128

GLA_HEADS = 4
GLA_DK = D_MODEL // (2 * GLA_HEADS)
GLA_DV = D_MODEL // GLA_HEADS
GLA_KEY = GLA_HEADS * GLA_DK
GLA_VAL = GLA_HEADS * GLA_DV
GLA_RANK = 16
GLA_TAU = 16.0

DN_HEADS = 8
DN_DK = D_MODEL // DN_HEADS
DN_DV = D_MODEL // DN_HEADS
DN_KEY = DN_HEADS * DN_DK
DN_VAL = DN_HEADS * DN_DV

HY_ORDER = 2
HY_SIDES = 2
HY_BANDS = 16
HY_SHIFT = 0.05

N_EXPERTS = 8
TOP_K = 2
MOE_TILE = 512

VMEM_LIMIT_BYTES = 56 * 1024 * 1024

_NT = (((1,), (1,)), ((), ()))
_TN = (((0,), (0,)), ((), ()))


def _cparams(sem):
    return pltpu.CompilerParams(dimension_semantics=sem, vmem_limit_bytes=VMEM_LIMIT_BYTES)


def _ln_rows(y, g, b):
    mu = jnp.mean(y, axis=-1, keepdims=True)
    yc = y - mu
    var = jnp.mean(yc * yc, axis=-1, keepdims=True)
    return yc * lax.rsqrt(var + LN_EPS) * g + b


def _mm_kernel(*refs, nk, mod, ln):
    refs = list(refs)
    a_ref, b_ref = refs[:2]
    o_ref = refs[-1]
    extra = refs[2:-1]
    a = a_ref[...]
    if mod:
        a = a * (1.0 + extra[0][0]) + extra[1][0]
        extra = extra[2:]
    part = jnp.dot(a.astype(BF16), b_ref[...], preferred_element_type=F32)
    if ln:
        res_ref, gate_ref, g_ref, bias_ref = extra
        o_ref[...] = _ln_rows(DEEPNORM_ALPHA * res_ref[...] + gate_ref[0] * part, g_ref[...], bias_ref[...])
    elif nk == 1:
        o_ref[...] = part.astype(o_ref.dtype)
    else:
        k = pl.program_id(2)

        @pl.when(k == 0)
        def _():
            o_ref[...] = part

        @pl.when(k > 0)
        def _():
            o_ref[...] += part


def _pick(n, cands):
    for c in cands:
        if n % c == 0:
            return c
    return n


def _lane_tile(n, cap):
    best = LANES
    for t in range(LANES, cap + 1, LANES):
        if n % t == 0:
            best = t
    return best


def _mm(a, w, mod=None, ln=None, out_dtype=F32):
    M, K = a.shape
    N = w.shape[1]
    n_pad = (-N) % LANES
    wb = w.astype(BF16)
    if n_pad:
        wb = jnp.pad(wb, ((0, 0), (0, n_pad)))
    Np = N + n_pad
    groups = mod[0].shape[0] if mod is not None else (ln[1].shape[0] if ln is not None else 1)
    rows = M // groups
    tm = _pick(rows, (1024, 512, 256, 128, 64, 32, 16, 8))
    tn = Np if ln is not None else _lane_tile(Np, 1408)
    tk = K if (mod is not None or ln is not None) else _pick(K, (1024, 512, 256, 128))
    nk = K // tk
    per_group = lambda width: pl.BlockSpec((1, 1, width), lambda i, j, k: ((i * tm) // rows, 0, 0))
    specs = [pl.BlockSpec((tm, tk), lambda i, j, k: (i, k)),
             pl.BlockSpec((tk, tn), lambda i, j, k: (k, j))]
    args = [a, wb]
    if mod is not None:
        specs += [per_group(K), per_group(K)]
        args += list(mod)
    if ln is not None:
        assert n_pad == 0
        row_vec = pl.BlockSpec((1, N), lambda i, j, k: (0, 0))
        specs += [pl.BlockSpec((tm, N), lambda i, j, k: (i, 0)), per_group(N), row_vec, row_vec]
        args += list(ln)
    return pl.pallas_call(
        functools.partial(_mm_kernel, nk=nk, mod=mod is not None, ln=ln is not None),
        grid=(M // tm, Np // tn, nk),
        in_specs=specs,
        out_specs=pl.BlockSpec((tm, tn), lambda i, j, k: (i, j)),
        out_shape=jax.ShapeDtypeStruct((M, Np), out_dtype),
        compiler_params=_cparams(("parallel", "parallel", "arbitrary")),
        name="mm",
    )(*args)


def _mm3(u, w, mod=None, out_dtype=F32):
    B, L, K = u.shape
    return _mm(u.reshape(B * L, K), w, mod=mod, out_dtype=out_dtype).reshape(B, L, -1)


def _ffn_kernel(*refs, nf, block, row_scale):
    refs = list(refs)
    u_ref, wg_ref, wu_ref, wd_ref = refs[1:5]
    o_ref, acc_ref = refs[-2:]
    extra = refs[5:-2]
    f = pl.program_id(1)

    @pl.when(f == 0)
    def _():
        acc_ref[...] = jnp.zeros_like(acc_ref)

    u = u_ref[...]
    if block:
        u = u * (1.0 + extra[0][0]) + extra[1][0]
    u = u.astype(BF16)
    g = jnp.dot(u, wg_ref[0], preferred_element_type=F32)
    up = jnp.dot(u, wu_ref[0], preferred_element_type=F32)
    act = (g * jax.nn.sigmoid(g) * up).astype(BF16)
    acc_ref[...] += jnp.dot(act, wd_ref[0], preferred_element_type=F32)

    @pl.when(f == nf - 1)
    def _():
        y = acc_ref[...]
        if row_scale:
            y = y * extra[0][...]
        if block:
            y = _ln_rows(DEEPNORM_ALPHA * u_ref[...] + extra[2][0] * y, extra[3][...], extra[4][...])
        o_ref[...] = y.astype(o_ref.dtype)


def _ffn(u, w_gu, w_down, tile_expert, tm, block=None, row_scale=None, out_dtype=F32):
    M, D = u.shape
    F = w_down.shape[1]
    tf = _pick(F, (1792, 1408, 512, 256, 128))
    nf = F // tf
    specs = [pl.BlockSpec((tm, D), lambda i, f, te: (i, 0)),
             pl.BlockSpec((1, D, tf), lambda i, f, te: (te[i], 0, f)),
             pl.BlockSpec((1, D, tf), lambda i, f, te: (te[i], 0, f + nf)),
             pl.BlockSpec((1, tf, D), lambda i, f, te: (te[i], f, 0))]
    args = [tile_expert, u, w_gu, w_gu, w_down]
    if block is not None:
        rows = M // block[0].shape[0]
        assert rows % tm == 0
        per_group = pl.BlockSpec((1, 1, D), lambda i, f, te: ((i * tm) // rows, 0, 0))
        row_vec = pl.BlockSpec((1, D), lambda i, f, te: (0, 0))
        specs += [per_group, per_group, per_group, row_vec, row_vec]
        args += list(block)
    if row_scale is not None:
        specs.append(pl.BlockSpec((tm, 1), lambda i, f, te: (i, 0)))
        args.append(row_scale)
    grid_spec = pltpu.PrefetchScalarGridSpec(
        num_scalar_prefetch=1,
        grid=(M // tm, nf),
        in_specs=specs,
        out_specs=pl.BlockSpec((tm, D), lambda i, f, te: (i, 0)),
        scratch_shapes=[pltpu.VMEM((tm, D), F32)])
    return pl.pallas_call(
        functools.partial(_ffn_kernel, nf=nf, block=block is not None, row_scale=row_scale is not None),
        grid_spec=grid_spec,
        out_shape=jax.ShapeDtypeStruct((M, D), out_dtype),
        compiler_params=_cparams(("parallel", "arbitrary")),
        name="ffn",
    )(*args)


def _grid_sincos(rows, dtype):
    r, col = jnp.meshgrid(jnp.arange(rows, dtype=F32), jnp.arange(GRID_W, dtype=F32), indexing='ij')
    n_freq = D_MODEL // 4
    omega = 1.0 / (10000.0 ** (jnp.arange(n_freq, dtype=F32) / n_freq))

    def emb(p):
        ang = p.reshape(-1)[:, None] * omega[None, :]
        return jnp.concatenate([jnp.sin(ang), jnp.cos(ang)], -1)

    return jnp.concatenate([emb(r), emb(col)], -1).astype(dtype)


def _bdot(a, b, dims=None):
    a = a.astype(BF16)
    b = b.astype(BF16)
    if dims is None:
        return jnp.dot(a, b, preferred_element_type=F32)
    return lax.dot_general(a, b, dims, preferred_element_type=F32)


GLA_CHUNK = 128
GLA_BPS = 4


def _scan_specs(nc, rev, widths_and_blocks):
    def cidx(j):
        return nc - 1 - j if rev else j
    return [pl.BlockSpec((GLA_BPS, GLA_CHUNK, w), functools.partial(lambda b, j, cb: (b, cidx(j), cb), cb=cb))
            for w, cb in widths_and_blocks]


def _head_readout(o_h, norm_g, gate):
    inv = lax.rsqrt(jnp.mean(o_h * o_h, axis=-1, keepdims=True) + NORM_EPS)
    gate = gate.astype(F32)
    return o_h * inv * norm_g * (gate * jax.nn.sigmoid(gate))


def _gla_kernel(*refs, rev, nc, with_prev, readout):
    refs = list(refs)
    q_ref, k_ref, v_ref, t_ref, w2_ref, gb_ref, s0_ref = refs[:7]
    o_ref, sfin_ref, s_scr = refs[-3:]
    extra = refs[7:-3]
    prev_ref = extra.pop(0) if with_prev else None
    gate_ref, ng_ref = extra if readout else (None, None)
    j = pl.program_id(1)

    @pl.when(j == 0)
    def _():
        s_scr[...] = s0_ref[...]

    row = lax.broadcasted_iota(jnp.int32, (GLA_CHUNK, GLA_CHUNK), 0)
    col = lax.broadcasted_iota(jnp.int32, (GLA_CHUNK, GLA_CHUNK), 1)
    incl = (col >= row) if rev else (col <= row)
    last = 0 if rev else GLA_CHUNK - 1
    mid = GLA_CHUNK // 2
    scale = GLA_DK ** -0.5
    chains = [(bi, h) for bi in range(GLA_BPS) for h in range(GLA_HEADS)]
    ks = {h: slice(h * GLA_DK, (h + 1) * GLA_DK) for h in range(GLA_HEADS)}
    vs = {h: slice(h * GLA_DV, (h + 1) * GLA_DV) for h in range(GLA_HEADS)}
    b_all = []
    for bi in range(GLA_BPS):
        z = _bdot(t_ref[bi], w2_ref[...]) + gb_ref[...]
        log_a = (jnp.minimum(z, 0.0) - jnp.log(1.0 + jnp.exp(-jnp.abs(z)))) * (1.0 / GLA_TAU)
        b_all.append(jnp.dot(incl.astype(F32), log_a, precision=HIGHEST, preferred_element_type=F32))
    bhs = [b_all[bi][:, ks[h]] for bi, h in chains]
    b_lasts = [bh[last:last + 1, :] for bh in bhs]
    b_mids = [bh[mid:mid + 1, :] for bh in bhs]
    qfs = [q_ref[bi, :, ks[h]].astype(F32) * scale for bi, h in chains]
    kfs = [k_ref[bi, :, ks[h]].astype(F32) for bi, h in chains]
    vbs = [v_ref[bi, :, vs[h]].astype(BF16) for bi, h in chains]
    sts = [s_scr[bi, h] for bi, h in chains]
    n = range(len(chains))
    q_piv = [(qfs[i] * jnp.exp(bhs[i] - b_mids[i])).astype(BF16) for i in n]
    k_piv = [(kfs[i] * jnp.exp(b_mids[i] - bhs[i])).astype(BF16) for i in n]
    scores = [jnp.where(incl, _bdot(q_piv[i], k_piv[i], _NT), 0.0) for i in n]
    inter = [_bdot(qfs[i] * jnp.exp(bhs[i]), sts[i], _NT) for i in n]
    intra = [_bdot(scores[i], vbs[i]) for i in n]
    kv = [_bdot(vbs[i], kfs[i] * jnp.exp(b_lasts[i] - bhs[i]), _TN) for i in n]
    for i, (bi, h) in enumerate(chains):
        s_scr[bi, h] = sts[i] * jnp.exp(b_lasts[i]) + kv[i]
        o_h = intra[i] + inter[i]
        if prev_ref is not None:
            o_h = o_h + prev_ref[bi, :, vs[h]]
        if readout:
            o_h = _head_readout(o_h, ng_ref[...], gate_ref[bi, :, vs[h]])
        o_ref[bi, :, vs[h]] = o_h.astype(o_ref.dtype)

    @pl.when(j == nc - 1)
    def _():
        sfin_ref[...] = s_scr[...]


def _gla_scan(proj, w2, gb, s0, prev, rev, norm_g=None):
    B, L, _ = proj.shape
    assert L % GLA_CHUNK == 0 and B % GLA_BPS == 0
    nc = L // GLA_CHUNK
    with_prev = prev is not None
    readout = norm_g is not None
    gate_cb = (2 * GLA_KEY + 2 * GLA_VAL) // LANES
    state = pl.BlockSpec((GLA_BPS, GLA_HEADS, GLA_DV, GLA_DK), lambda b, j: (b, 0, 0, 0))
    specs = _scan_specs(nc, rev, [(GLA_KEY, 0), (GLA_KEY, 1), (GLA_VAL, 1), (LANES, gate_cb)])
    specs += [pl.BlockSpec((LANES, GLA_KEY), lambda b, j: (0, 0)),
              pl.BlockSpec((1, GLA_KEY), lambda b, j: (0, 0)),
              state]
    args = [proj, proj, proj, proj, w2, gb, s0]
    o_spec = _scan_specs(nc, rev, [(GLA_VAL, 0)])[0]
    if with_prev:
        specs.append(o_spec)
        args.append(prev)
    if readout:
        specs += [_scan_specs(nc, rev, [(GLA_VAL, 2)])[0], pl.BlockSpec((1, GLA_DV), lambda b, j: (0, 0))]
        args += [proj, norm_g.reshape(1, GLA_DV)]
    return pl.pallas_call(
        functools.partial(_gla_kernel, rev=rev, nc=nc, with_prev=with_prev, readout=readout),
        grid=(B // GLA_BPS, nc),
        in_specs=specs,
        out_specs=[o_spec, state],
        out_shape=[jax.ShapeDtypeStruct((B, L, GLA_VAL), BF16 if readout else F32),
                   jax.ShapeDtypeStruct((B, GLA_HEADS, GLA_DV, GLA_DK), F32)],
        scratch_shapes=[pltpu.VMEM((GLA_BPS, GLA_HEADS, GLA_DV, GLA_DK), F32)],
        compiler_params=_cparams(("parallel", "arbitrary")),
        name="gla_scan_rev" if rev else "gla_scan_fwd",
    )(*args)


def _gla_mixer(x_c, mod_c, x_x, mod_x, w_in, gate_w1, gate_w2, gate_b, norm_g):
    w_ext = jnp.concatenate([w_in, gate_w1[0], gate_w1[1]], axis=1)
    w2 = []
    for d in range(2):
        w2.append(jnp.zeros((LANES, GLA_KEY), F32).at[d * GLA_RANK:(d + 1) * GLA_RANK].set(gate_w2[d]).astype(BF16))
    gb = gate_b[:, None, :]

    def bidir(proj, s0f, s0b):
        o, s_f = _gla_scan(proj, w2[0], gb[0], s0f, None, False)
        y, s_b = _gla_scan(proj, w2[1], gb[1], s0b, o, True, norm_g=norm_g)
        return y, s_f, s_b

    s0 = jnp.zeros((x_c.shape[0], GLA_HEADS, GLA_DV, GLA_DK), F32)
    y_c, s_f, s_b = bidir(_mm3(x_c, w_ext, mod=mod_c, out_dtype=BF16), s0, s0)
    y_x, _, _ = bidir(_mm3(x_x, w_ext, mod=mod_x, out_dtype=BF16), s_f, s_b)
    return y_c, y_x


def _dn_prep_kernel(x_ref, w_ref, o_ref, *, L):
    cb = pl.program_id(1)
    y = _dwconv3_rows(x_ref[0], w_ref[...])
    y = y * jax.nn.sigmoid(y)
    inv = lax.rsqrt(jnp.sum(y * y, axis=1, keepdims=True) + NORM_EPS)
    q_scale = jnp.where(cb < DN_HEADS, DN_DK ** -0.5, 1.0)
    fac = jnp.where(cb < 2 * DN_HEADS, inv * q_scale, 1.0)
    o_ref[0] = (y * fac).astype(BF16)


def _dn_prep(proj, conv_w):
    B, L, _ = proj.shape
    ncb = (2 * DN_KEY + DN_VAL) // LANES
    return pl.pallas_call(
        functools.partial(_dn_prep_kernel, L=L),
        grid=(B, ncb),
        in_specs=[pl.BlockSpec((1, L, LANES), lambda b, c: (b, 0, c)),
                  pl.BlockSpec((conv_w.shape[0], LANES), lambda b, c: (0, c))],
        out_specs=pl.BlockSpec((1, L, LANES), lambda b, c: (b, 0, c)),
        out_shape=jax.ShapeDtypeStruct((B, L, ncb * LANES), BF16),
        compiler_params=_cparams(("parallel", "parallel")),
        name="dn_prep",
    )(proj, conv_w)


DN_GROUP = 4
DN_ROWS = DN_GROUP * CHUNK


DN_CPS = 2
DN_BPS = 4
DN_LEVELS = CHUNK.bit_length() - 1
M_INCL, M_STRICT, M_EYE, M_LEVEL0 = 0, 1, 2, 3


def _dn_masks(rev):
    r = jnp.arange(DN_ROWS, dtype=jnp.int32)[:, None]
    c = jnp.arange(DN_ROWS, dtype=jnp.int32)[None, :]
    same_head = (r // CHUNK) == (c // CHUNK)
    rt, ct = r % CHUNK, c % CHUNK
    out = [same_head & ((ct >= rt) if rev else (ct <= rt)), same_head & ((ct > rt) if rev else (ct < rt)), r == c]
    for lv in range(DN_LEVELS):
        s = 1 << lv
        r_hi, c_hi = (r // s) % 2 == 1, (c // s) % 2 == 1
        out.append(((r // (2 * s)) == (c // (2 * s))) & ((c_hi & ~r_hi) if rev else (r_hi & ~c_hi)))
    return jnp.stack(out).astype(F32)


def _dn_chunk_kernel(q_ref, k_ref, v_ref, ba_ref, par_ref, m_ref, p_ref, qm_ref, g_ref, r_ref, oi_ref, *, rev):
    prow = lax.broadcasted_iota(jnp.int32, (2 * CHUNK, CHUNK), 0)
    pcol = lax.broadcasted_iota(jnp.int32, (2 * CHUNK, CHUNK), 1)
    tri_pad = (((pcol >= prow) if rev else (pcol <= prow)) & (prow < CHUNK)).astype(F32)
    d = 1 if rev else 0
    last = 0 if rev else CHUNK - 1
    incl = m_ref[M_INCL]
    eye = m_ref[M_EYE]
    items = []
    for ci in range(DN_CPS):
        tr = slice(ci * CHUNK, (ci + 1) * CHUNK)
        ba = ba_ref[0, tr, :]
        beta_all = jax.nn.sigmoid(ba)
        sp_in = ba + par_ref[1:2, :]
        softplus = jnp.maximum(sp_in, 0.0) + jnp.log(1.0 + jnp.exp(-jnp.abs(sp_in)))
        g_all = par_ref[0:1, :] * softplus
        gc_pad = jnp.dot(tri_pad, g_all, precision=HIGHEST, preferred_element_type=F32)
        gc = gc_pad[:CHUNK]
        gc_t = gc_pad.T
        for grp in range(DN_HEADS // DN_GROUP):
            heads = range(grp * DN_GROUP, (grp + 1) * DN_GROUP)
            gls = [2 * DN_HEADS + d * DN_HEADS + h for h in heads]
            qs = jnp.concatenate([q_ref[0, tr, h * DN_DK:(h + 1) * DN_DK] for h in heads], axis=0)
            ks = jnp.concatenate([k_ref[0, tr, h * DN_DK:(h + 1) * DN_DK] for h in heads], axis=0)
            vs = jnp.concatenate([v_ref[0, tr, h * DN_DV:(h + 1) * DN_DV] for h in heads], axis=0)
            beta = jnp.concatenate([beta_all[:, d * DN_HEADS + h:d * DN_HEADS + h + 1] for h in heads], axis=0)
            gcol = jnp.concatenate([gc[:, gl:gl + 1] for gl in gls], axis=0)
            glast = jnp.concatenate([jnp.broadcast_to(gc[last:last + 1, gl:gl + 1], (CHUNK, 1)) for gl in gls],
                                    axis=0)
            rows = [gc_t[gl:gl + 1, :] for gl in gls]
            grow = jnp.concatenate([rows[0] + pltpu.roll(rows[1], CHUNK, 1),
                                    rows[2] + pltpu.roll(rows[3], CHUNK, 1)], axis=1)
            items.append(dict(ci=ci, tr=tr, heads=heads, qs=qs, ks=ks, vs=vs, beta=beta, gcol=gcol, glast=glast,
                              grow=grow, gfin=[gc[last:last + 1, gl:gl + 1] for gl in gls]))
    dms = [jnp.exp(jnp.minimum(it["gcol"] - it["grow"], 0.0)) * incl for it in items]
    kfs = [it["ks"].astype(F32) for it in items]
    kbetas = [kf * it["beta"] for kf, it in zip(kfs, items)]
    a_s = [_bdot(kb, it["ks"], _NT) * (dm * m_ref[M_STRICT]) for kb, it, dm in zip(kbetas, items, dms)]
    ts = [eye - a * m_ref[M_LEVEL0] for a in a_s]
    for lv in range(1, DN_LEVELS):
        mts = [_bdot(a * m_ref[M_LEVEL0 + lv], t) for a, t in zip(a_s, ts)]
        ts = [t - _bdot(t, mt) for t, mt in zip(ts, mts)]
    rhs = [jnp.concatenate([it["vs"].astype(F32) * it["beta"], kb * jnp.exp(it["gcol"])], axis=1)
           for it, kb in zip(items, kbetas)]
    sols = [_bdot(t, r) for t, r in zip(ts, rhs)]
    qks = [_bdot(it["qs"], it["ks"], _NT) * dm for it, dm in zip(items, dms)]
    qsols = [_bdot(qk, sol) for qk, sol in zip(qks, sols)]
    for it, kf, sol, qsol in zip(items, kfs, sols, qsols):
        ci, tr = it["ci"], it["tr"]
        k_dec = kf * jnp.exp(it["glast"] - it["gcol"])
        r_all = it["qs"].astype(F32) * jnp.exp(it["gcol"]) - qsol[:, DN_DV:]
        for i, h in enumerate(it["heads"]):
            rs = slice(i * CHUNK, (i + 1) * CHUNK)
            hs = slice(h * DN_DV, (h + 1) * DN_DV)
            pq = _bdot(k_dec[rs], sol[rs], _TN)
            qm_ref[0, ci, h] = pq[:, :DN_DV]
            p_ref[0, ci, h] = (-pq[:, DN_DV:]).astype(BF16)
            g_ref[0, ci, h:h + 1, :] = jnp.broadcast_to(jnp.exp(it["gfin"][i]), (1, DN_DV))
            r_ref[0, tr, hs] = r_all[rs].astype(BF16)
            oi_ref[0, tr, hs] = qsol[rs, :DN_DV]


def _dn_chunk(qkv, proj, par, rev):
    B, L, _ = qkv.shape
    nc = L // CHUNK
    assert nc % DN_CPS == 0
    ba_cb = (2 * DN_KEY + 2 * DN_VAL) // LANES
    tok = lambda w, cb: pl.BlockSpec((1, DN_CPS * CHUNK, w), lambda b, j: (b, j, cb))
    mat = pl.BlockSpec((1, DN_CPS, DN_HEADS, DN_DK, DN_DV), lambda b, j: (b, j, 0, 0, 0))
    masks = _dn_masks(rev)
    return pl.pallas_call(
        functools.partial(_dn_chunk_kernel, rev=rev),
        grid=(B, nc // DN_CPS),
        in_specs=[tok(DN_KEY, 0), tok(DN_KEY, 1), tok(DN_VAL, 2), tok(LANES, ba_cb),
                  pl.BlockSpec((2, LANES), lambda b, j: (0, 0)),
                  pl.BlockSpec(masks.shape, lambda b, j: (0, 0, 0), pipeline_mode=pl.Buffered(1))],
        out_specs=[mat, mat, pl.BlockSpec((1, DN_CPS, DN_HEADS, DN_DV), lambda b, j: (b, j, 0, 0)),
                   tok(DN_KEY, 0), tok(DN_VAL, 0)],
        out_shape=[jax.ShapeDtypeStruct((B, nc, DN_HEADS, DN_DK, DN_DV), BF16),
                   jax.ShapeDtypeStruct((B, nc, DN_HEADS, DN_DK, DN_DV), F32),
                   jax.ShapeDtypeStruct((B, nc, DN_HEADS, DN_DV), F32),
                   jax.ShapeDtypeStruct((B, L, DN_KEY), BF16),
                   jax.ShapeDtypeStruct((B, L, DN_VAL), F32)],
        compiler_params=_cparams(("parallel", "parallel")),
        name="dn_chunk_rev" if rev else "dn_chunk_fwd",
    )(qkv, qkv, qkv, proj, par, masks)


def _dn_seq_kernel(*refs, nc, with_prev, readout):
    refs = list(refs)
    p_ref, qm_ref, g_ref, r_ref, oi_ref, s0_ref = refs[:6]
    o_ref, sfin_ref, s_scr = refs[-3:]
    extra = refs[6:-3]
    prev_ref = extra.pop(0) if with_prev else None
    gate_ref, ng_ref = extra if readout else (None, None)
    j = pl.program_id(1)

    @pl.when(j == 0)
    def _():
        s_scr[...] = s0_ref[...]

    chains = [(bi, h) for bi in range(DN_BPS) for h in range(DN_HEADS)]
    states = [s_scr[bi, h] for bi, h in chains]
    xs = []
    for i, (bi, h) in enumerate(chains):
        lhs = jnp.concatenate([p_ref[bi, 0, h], r_ref[bi, :, h * DN_DK:(h + 1) * DN_DK]], axis=0)
        xs.append(_bdot(lhs, states[i]))
    for i, (bi, h) in enumerate(chains):
        hs = slice(h * DN_DV, (h + 1) * DN_DV)
        s_scr[bi, h] = g_ref[bi, 0, h:h + 1, :] * states[i] + xs[i][:DN_DK] + qm_ref[bi, 0, h]
        o_h = xs[i][DN_DK:] + oi_ref[bi, :, hs]
        if prev_ref is not None:
            o_h = o_h + prev_ref[bi, :, hs]
        if readout:
            o_h = _head_readout(o_h, ng_ref[...], gate_ref[bi, :, hs])
        o_ref[bi, :, hs] = o_h.astype(o_ref.dtype)

    @pl.when(j == nc - 1)
    def _():
        sfin_ref[...] = s_scr[...]


def _dn_seq(chunk_terms, s0, prev, rev, readout=None):
    p, qm, g, r, oi = chunk_terms
    B, L, _ = r.shape
    nc = L // CHUNK
    with_prev = prev is not None

    def cidx(j):
        return nc - 1 - j if rev else j

    assert B % DN_BPS == 0
    mat = pl.BlockSpec((DN_BPS, 1, DN_HEADS, DN_DK, DN_DV), lambda b, j: (b, cidx(j), 0, 0, 0))
    tok = pl.BlockSpec((DN_BPS, CHUNK, DN_VAL), lambda b, j: (b, cidx(j), 0))
    state = pl.BlockSpec((DN_BPS, DN_HEADS, DN_DK, DN_DV), lambda b, j: (b, 0, 0, 0))
    specs = [mat, mat, pl.BlockSpec((DN_BPS, 1, DN_HEADS, DN_DV), lambda b, j: (b, cidx(j), 0, 0)), tok, tok, state]
    args = [p, qm, g, r, oi, s0]
    if with_prev:
        specs.append(tok)
        args.append(prev)
    if readout is not None:
        z_cb = (2 * DN_KEY + DN_VAL) // DN_VAL
        specs += [pl.BlockSpec((DN_BPS, CHUNK, DN_VAL), lambda b, j: (b, cidx(j), z_cb)),
                  pl.BlockSpec((1, DN_DV), lambda b, j: (0, 0))]
        args += [readout[0], readout[1].reshape(1, DN_DV)]
    return pl.pallas_call(
        functools.partial(_dn_seq_kernel, nc=nc, with_prev=with_prev, readout=readout is not None),
        grid=(B // DN_BPS, nc),
        in_specs=specs,
        out_specs=[tok, state],
        out_shape=[jax.ShapeDtypeStruct((B, L, DN_VAL), BF16 if readout is not None else F32),
                   jax.ShapeDtypeStruct((B, DN_HEADS, DN_DK, DN_DV), F32)],
        scratch_shapes=[pltpu.VMEM((DN_BPS, DN_HEADS, DN_DK, DN_DV), F32)],
        compiler_params=_cparams(("parallel", "arbitrary")),
        name="dn_seq_rev" if rev else "dn_seq_fwd",
    )(*args)


def _delta_mixer(x_c, mod_c, x_x, mod_x, w_in, conv_w, A_log, dt_bias, norm_g):
    a_lanes = slice(2 * DN_HEADS, 4 * DN_HEADS)
    par = jnp.zeros((2, LANES), F32)
    par = par.at[0, a_lanes].set(-jnp.exp(A_log).reshape(-1)).at[1, a_lanes].set(dt_bias.reshape(-1))

    def bidir(proj, s0f, s0b):
        qkv = _dn_prep(proj, conv_w)
        o, s_f = _dn_seq(_dn_chunk(qkv, proj, par, False), s0f, None, False)
        y, s_b = _dn_seq(_dn_chunk(qkv, proj, par, True), s0b, o, True, readout=(proj, norm_g))
        return y, s_f, s_b

    s0 = jnp.zeros((x_c.shape[0], DN_HEADS, DN_DK, DN_DV), F32)
    y_c, s_f, s_b = bidir(_mm3(x_c, w_in, mod=mod_c), s0, s0)
    y_x, _, _ = bidir(_mm3(x_x, w_in, mod=mod_x), s_f, s_b)
    return y_c, y_x


def _hyena_filter_spectrum(L, pe_w1, pe_b1, freq, pe_w2, pe_b2, pe_w3, decay):
    t = jnp.arange(L, dtype=F32)
    tn = t / L
    ang = (2.0 * math.pi / L) * t[:, None] * jnp.arange(1, HY_BANDS + 1, dtype=F32)[None, :]
    feats = jnp.concatenate([tn[:, None], jnp.cos(ang), jnp.sin(ang)], -1)
    h = jnp.sin(freq * (feats @ pe_w1 + pe_b1))
    h = jnp.sin(freq * (h @ pe_w2 + pe_b2))
    h = (h @ pe_w3) * (jnp.exp(-tn[:, None] * decay) + HY_SHIFT)
    h = h.reshape(L, HY_ORDER, HY_SIDES, D_MODEL)
    h_circ = jnp.concatenate([h[:, :, 0], jnp.zeros((1, HY_ORDER, D_MODEL), F32), h[:0:-1, :, 1]], axis=0)
    return jnp.fft.rfft(h_circ, axis=0)


HY_TD = 256
HY_TK = 512


def _dwconv3_rows(x, w):
    L = x.shape[0]
    rowi = lax.broadcasted_iota(jnp.int32, x.shape, 0)
    x_prev = jnp.where(rowi == 0, 0.0, pltpu.roll(x, 1, 0))
    x_next = jnp.where(rowi == L - 1, 0.0, pltpu.roll(x, L - 1, 0))
    return w[0:1] * x_prev + w[1:2] * x + w[2:3] * x_next


def _hy_tables(L):
    n = 2 * L
    k = jnp.arange(L, dtype=jnp.int32)
    ang = (2.0 * math.pi / n) * ((k[:, None] * k[None, :]) % n).astype(F32)
    sign = 1.0 - 2.0 * (k % 2).astype(F32)
    ws = (-jnp.sin(ang)).at[0, :].set(sign)
    return jnp.cos(ang).astype(BF16), ws.astype(BF16), ws.T.astype(BF16)


def _hy_filter(L, filt):
    spec = _hyena_filter_spectrum(L, *filt)
    n = 2 * L
    scale = jnp.full((L, 1, 1), 2.0 / n, F32).at[0].set(1.0 / n)
    hr = jnp.real(spec[:L]) * scale
    hi = (jnp.imag(spec[:L]) * scale).at[0].set(0.0)
    hr2 = hr.at[0].set(jnp.real(spec[L]) / n)
    return tuple(jnp.transpose(a, (1, 0, 2)) for a in (hr, hi, hr2))


def _hy_fwd_kernel(x_ref, cw_ref, c_ref, ws_ref, hr_ref, hi_ref, hr2_ref, yr_ref, yi_ref, *, L, conv):
    x = x_ref[0].astype(F32)
    vb = (_dwconv3_rows(x, cw_ref[...]) if conv else x).astype(BF16)
    tk = min(HY_TK, L)
    for kb in range(L // tk):
        rows = slice(kb * tk, (kb + 1) * tk)
        ur = jnp.dot(c_ref[rows, :], vb, preferred_element_type=F32)
        ui = jnp.dot(ws_ref[rows, :], vb, preferred_element_type=F32)
        hr = hr_ref[rows, :]
        hi = hi_ref[rows, :]
        yr_ref[0, rows, :] = (ur * hr - ui * hi).astype(BF16)
        yi_ref[0, rows, :] = (ur * hi + ui * hr2_ref[rows, :]).astype(BF16)


def _hy_inv_kernel(yr_ref, yi_ref, c_ref, wst_ref, g_ref, gw_ref, v_ref, vw_ref, sk_ref, o_ref, *, L, conv_v):
    gate = _dwconv3_rows(g_ref[0].astype(F32), gw_ref[...])
    v = v_ref[0].astype(F32)
    if conv_v:
        v = _dwconv3_rows(v, vw_ref[...])
    yr = yr_ref[0]
    yi = yi_ref[0]
    tk = min(HY_TK, L)
    for tb in range(L // tk):
        rows = slice(tb * tk, (tb + 1) * tk)
        y = (jnp.dot(c_ref[rows, :], yr, preferred_element_type=F32)
             + jnp.dot(wst_ref[rows, :], yi, preferred_element_type=F32))
        o_ref[0, rows, :] = (gate[rows, :] * (y + sk_ref[...] * v[rows, :])).astype(o_ref.dtype)


def _hy_specs(L, td):
    table = pl.BlockSpec((L, L), lambda d, b: (0, 0), pipeline_mode=pl.Buffered(1))
    col = lambda off: pl.BlockSpec((1, L, td), functools.partial(lambda d, b, off: (b, 0, off + d), off=off))
    par = lambda r: pl.BlockSpec((r, td), lambda d, b: (0, d))
    return table, col, par


def _hy_fwd(x, col_off, conv_w, c, ws, hr, hi, hr2):
    B, L, _ = x.shape
    D = hr.shape[1]
    td = HY_TD
    table, col, par = _hy_specs(L, td)
    conv = conv_w is not None
    if not conv:
        conv_w = jnp.zeros((3, D), F32)
    out = jax.ShapeDtypeStruct((B, L, D), BF16)
    return pl.pallas_call(
        functools.partial(_hy_fwd_kernel, L=L, conv=conv),
        grid=(D // td, B),
        in_specs=[col(col_off), par(3), table, table, par(L), par(L), par(L)],
        out_specs=[col(0), col(0)],
        out_shape=[out, out],
        compiler_params=_cparams(("parallel", "parallel")),
        name="hy_fwd",
    )(x, conv_w, c, ws, hr, hi, hr2)


def _hy_inv(yr, yi, c, wst, proj, gate_off, gate_w, sig, sig_off, sig_w, skip, out_dtype=F32):
    B, L, D = yr.shape
    td = HY_TD
    table, col, par = _hy_specs(L, td)
    conv_v = sig_w is not None
    if not conv_v:
        sig_w = jnp.zeros((3, D), F32)
    return pl.pallas_call(
        functools.partial(_hy_inv_kernel, L=L, conv_v=conv_v),
        grid=(D // td, B),
        in_specs=[col(0), col(0), table, table, col(gate_off), par(3), col(sig_off), par(3), par(1)],
        out_specs=col(0),
        out_shape=jax.ShapeDtypeStruct((B, L, D), out_dtype),
        compiler_params=_cparams(("parallel", "parallel")),
        name="hy_inv",
    )(yr, yi, c, wst, proj, gate_w, sig, sig_w, skip)


def _hyena_seq(x, mod, w_in, conv_w, filt, skip):
    B, L, D = x.shape
    assert conv_w.shape[0] == 3 and D % HY_TD == 0
    nd = D // HY_TD
    proj = _mm3(x, w_in, mod=mod, out_dtype=BF16)
    cw = [conv_w[:, i * D:(i + 1) * D] for i in range(3)]
    c, ws, wst = _hy_tables(L)
    hr, hi, hr2 = _hy_filter(L, filt)
    yr, yi = _hy_fwd(proj, 2 * nd, cw[2], c, ws, hr[0], hi[0], hr2[0])
    z = _hy_inv(yr, yi, c, wst, proj, 0, cw[0], proj, 2 * nd, cw[2], skip[0:1])
    yr, yi = _hy_fwd(z, 0, None, c, ws, hr[1], hi[1], hr2[1])
    return _hy_inv(yr, yi, c, wst, proj, nd, cw[1], z, 0, None, skip[1:2], out_dtype=BF16)


def _dense_block(x, mod, gate, ln_g, ln_b, w_gu, w_down):
    B, L, D = x.shape
    M = B * L
    tm = _pick(M // mod[0].shape[0], (512, 256, 128))
    te = jnp.zeros((M // tm,), jnp.int32)
    block = (mod[0], mod[1], gate, ln_g, ln_b)
    return _ffn(x.reshape(M, D), w_gu.astype(BF16)[None], w_down.astype(BF16)[None], te, tm,
                block=block).reshape(B, L, D)


def _route_kernel(x_ref, sc_ref, sh_ref, r_ref, h_ref, lg_ref):
    h = x_ref[...] * (1.0 + sc_ref[0]) + sh_ref[0]
    h_ref[...] = h.astype(BF16)
    lg_ref[...] = jnp.dot(h, r_ref[...], precision=HIGHEST, preferred_element_type=F32)


def _route(x, mod, router):
    M, D = x.shape
    rows = M // mod[0].shape[0]
    tm = _pick(rows, (1024, 512, 256, 128))
    r_pad = jnp.pad(router, ((0, 0), (0, LANES - router.shape[1])))
    per_group = pl.BlockSpec((1, 1, D), lambda i: ((i * tm) // rows, 0, 0))
    return pl.pallas_call(
        _route_kernel,
        grid=(M // tm,),
        in_specs=[pl.BlockSpec((tm, D), lambda i: (i, 0)), per_group, per_group,
                  pl.BlockSpec((D, LANES), lambda i: (0, 0))],
        out_specs=[pl.BlockSpec((tm, D), lambda i: (i, 0)), pl.BlockSpec((tm, LANES), lambda i: (i, 0))],
        out_shape=[jax.ShapeDtypeStruct((M, D), BF16), jax.ShapeDtypeStruct((M, LANES), F32)],
        compiler_params=_cparams(("parallel",)),
        name="moe_route",
    )(x, mod[0], mod[1], r_pad)


def _combine_ln_kernel(x_ref, y0_ref, y1_ref, gate_ref, g_ref, b_ref, o_ref):
    y = y0_ref[...].astype(F32) + y1_ref[...].astype(F32)
    o_ref[...] = _ln_rows(DEEPNORM_ALPHA * x_ref[...] + gate_ref[0] * y, g_ref[...], b_ref[...])


def _combine_ln(x, y0, y1, gate, ln_g, ln_b):
    M, D = x.shape
    rows = M // gate.shape[0]
    tm = _pick(rows, (1024, 512, 256, 128))
    tile = pl.BlockSpec((tm, D), lambda i: (i, 0))
    row_vec = pl.BlockSpec((1, D), lambda i: (0, 0))
    return pl.pallas_call(
        _combine_ln_kernel,
        grid=(M // tm,),
        in_specs=[tile, tile, tile, pl.BlockSpec((1, 1, D), lambda i: ((i * tm) // rows, 0, 0)), row_vec, row_vec],
        out_specs=tile,
        out_shape=jax.ShapeDtypeStruct((M, D), F32),
        compiler_params=_cparams(("parallel",)),
        name="moe_combine_ln",
    )(x, y0, y1, gate, ln_g, ln_b)


def _moe_block(x, mod, gate, ln_g, ln_b, router, w_gu, w_down):
    B, L, D = x.shape
    M = B * L
    tm = MOE_TILE
    xf = x.reshape(M, D)
    hb, logits = _route(xf, mod, router)
    top_val, top_idx = lax.top_k(logits[:, :N_EXPERTS], TOP_K)
    weights = jax.nn.softmax(top_val, axis=-1)
    e_flat = top_idx.reshape(-1).astype(jnp.int32)
    onehot = (e_flat[:, None] == jnp.arange(N_EXPERTS, dtype=jnp.int32)[None, :]).astype(jnp.int32)
    rank = jnp.sum((jnp.cumsum(onehot, axis=0) - onehot) * onehot, axis=1)
    counts = jnp.sum(onehot, axis=0)
    padded = ((counts + tm - 1) // tm) * tm
    ends = jnp.cumsum(padded)
    pos = (ends - padded)[e_flat] + rank
    P = M * TOP_K + N_EXPERTS * tm
    slot = jnp.zeros((P,), jnp.int32).at[pos].set(jnp.arange(M * TOP_K, dtype=jnp.int32))
    tile_expert = jnp.minimum(jnp.searchsorted(ends, jnp.arange(P // tm, dtype=jnp.int32) * tm, side='right'),
                              N_EXPERTS - 1).astype(jnp.int32)
    xs = hb[slot // TOP_K]
    row_w = weights.reshape(-1)[slot][:, None]
    ys = _ffn(xs, w_gu.astype(BF16), w_down.astype(BF16), tile_expert, tm, row_scale=row_w, out_dtype=BF16)
    pos2 = pos.reshape(M, TOP_K)
    return _combine_ln(xf, ys[pos2[:, 0]], ys[pos2[:, 1]], gate, ln_g, ln_b).reshape(B, L, D)


def kernel(x, c, ctx, c_ctx, ada_w, ada_b, ln_g, ln_b,
           gla_w_in, gla_gate_w1, gla_gate_w2, gla_gate_b, gla_norm_g, gla_w_out,
           dn_w_in, dn_conv, dn_A_log, dn_dt_bias, dn_norm_g, dn_w_out,
           hy_w_in, hy_conv, hy_pe_w1, hy_pe_b1, hy_freq, hy_pe_w2, hy_pe_b2, hy_pe_w3,
           hy_decay, hy_skip, hy_w_out,
           ffn_w_gu, ffn_w_down, moe_router, moe_w_gu, moe_w_down):
    B, L, D = x.shape
    x = x + _grid_sincos(L // GRID_W, x.dtype)[None]
    xc = ctx
    cond_x = jax.nn.silu(c)
    cond_c = jax.nn.silu(c_ctx)[None, :]

    def project_out(y, w_out, res, gate, g, b):
        Bn, Ln, _ = res.shape
        return _mm(y.reshape(Bn * Ln, -1), w_out, ln=(res.reshape(Bn * Ln, D), gate, g, b)).reshape(Bn, Ln, D)

    for i in range(DEPTH):
        last = i == DEPTH - 1
        kind = i % N_MIXERS
        j = i // N_MIXERS
        mx = (jnp.dot(cond_x, ada_w[i], precision=HIGHEST) + ada_b[i]).reshape(B, 6, 1, D)
        mc = (jnp.dot(cond_c, ada_w[i], precision=HIGHEST) + ada_b[i]).reshape(1, 6, 1, D)
        sh1, sc1, g1, sh2, sc2, g2 = (mx[:, t] for t in range(6))
        csh1, csc1, cg1, csh2, csc2, cg2 = (mc[:, t] for t in range(6))
        ln1 = (ln_g[i, 0][None, :], ln_b[i, 0][None, :])
        ln2 = (ln_g[i, 1][None, :], ln_b[i, 1][None, :])
        if kind == 0:
            y_c, y_x = _gla_mixer(xc, (csc1, csh1), x, (sc1, sh1), gla_w_in[j], gla_gate_w1[j], gla_gate_w2[j],
                                  gla_gate_b[j], gla_norm_g[j])
            w_out = gla_w_out[j]
        elif kind == 1:
            y_c, y_x = _delta_mixer(xc, (csc1, csh1), x, (sc1, sh1), dn_w_in[j], dn_conv[j], dn_A_log[j],
                                    dn_dt_bias[j], dn_norm_g[j])
            w_out = dn_w_out[j]
        else:
            filt = (hy_pe_w1[j], hy_pe_b1[j], hy_freq[j], hy_pe_w2[j], hy_pe_b2[j], hy_pe_w3[j], hy_decay[j])
            y_x = _hyena_seq(x, (sc1, sh1), hy_w_in[j], hy_conv[j], filt, hy_skip[j])
            y_c = None if last else _hyena_seq(xc, (csc1, csh1), hy_w_in[j], hy_conv[j], filt, hy_skip[j])
            w_out = hy_w_out[j]
        x = project_out(y_x, w_out, x, g1, *ln1)
        if not last:
            xc = project_out(y_c, w_out, xc, cg1, *ln1)

        if i % 2 == 0:
            x = _dense_block(x, (sc2, sh2), g2, *ln2, ffn_w_gu[i // 2], ffn_w_down[i // 2])
            if not last:
                xc = _dense_block(xc, (csc2, csh2), cg2, *ln2, ffn_w_gu[i // 2], ffn_w_down[i // 2])
        else:
            moe = (moe_router[i // 2], moe_w_gu[i // 2], moe_w_down[i // 2])
            x = _moe_block(x, (sc2, sh2), g2, *ln2, *moe)
            if not last:
                xc = _moe_block(xc, (csc2, csh2), cg2, *ln2, *moe)
    return x
```
